```python
import math
import jax, jax.numpy as jnp
from jax import lax
import numpy as np

D_MODEL = 2048
BATCH = 16
SEQ = 256
DEPTH = 4
DEC_BATCH = 4
DEC_SEQ = 2048
PAST_LEN = 512

GRID_W = 64
HEAD_DIM = 128
Q_BLOCK = 128
EPS = 1e-6
ROPE_THETA = 10000.0
DA_HEADS = 4
DA_DH = HEAD_DIM // 2
LRU_WIDTH = 512
LRU_BLOCKS = 4
LRU_BS = LRU_WIDTH // LRU_BLOCKS
LRU_C = 8.0
CONV_W = 4
CONV_LEFT = 2
NA_HEADS = 4
NA_ROWS = 8
NA_COLS = 16
GQA_Q_HEADS = 4
GQA_KV_HEADS = 2
IN_SPLITS = (DA_HEADS * HEAD_DIM, DA_HEADS * HEAD_DIM, DA_HEADS * HEAD_DIM,
             LRU_WIDTH, LRU_WIDTH,
             NA_HEADS * HEAD_DIM, NA_HEADS * HEAD_DIM, NA_HEADS * HEAD_DIM,
             GQA_Q_HEADS * HEAD_DIM, GQA_KV_HEADS * HEAD_DIM, GQA_KV_HEADS * HEAD_DIM)
IN_WIDTH = sum(IN_SPLITS)
MIX_WIDTH = DA_HEADS * HEAD_DIM + LRU_WIDTH + NA_HEADS * HEAD_DIM + GQA_Q_HEADS * HEAD_DIM
FFN_HIDDEN = ((8 * D_MODEL // 3 + 255) // 256) * 256

kernel_name = "hybrid_diffusion_parallel_heads_step"


def rmsnorm(x, g=None):
    xf = x.astype(jnp.float32)
    y = xf * lax.rsqrt(jnp.mean(xf * xf, axis=-1, keepdims=True) + EPS)
    if g is not None:
        y = y * g.astype(jnp.float32)
    return y.astype(x.dtype)


def axial_rope(x):
    n = x.shape[1]
    m = x.shape[-1] // 2
    t = jnp.arange(n)
    freqs = ROPE_THETA ** (-jnp.arange(0, m, 2, dtype=jnp.float32) / m)
    bshape = (n,) + (1,) * (x.ndim - 3) + (m // 2,)

    def rot(xh, pos):
        ang = (pos.astype(jnp.float32)[:, None] * freqs).reshape(bshape)
        cos = jnp.cos(ang).astype(x.dtype)
        sin = jnp.sin(ang).astype(x.dtype)
        x1, x2 = jnp.split(xh, 2, axis=-1)
        return jnp.concatenate([x1 * cos - x2 * sin, x1 * sin + x2 * cos], axis=-1)

    return jnp.concatenate([rot(x[..., :m], t // GRID_W), rot(x[..., m:], t % GRID_W)], axis=-1)


def sweep_query_blocks(fn, q):
    b, sq = q.shape[:2]
    nb = sq // Q_BLOCK
    qb = jnp.moveaxis(q.reshape((b, nb, Q_BLOCK) + q.shape[2:]), 1, 0)
    out = jnp.moveaxis(lax.map(fn, qb), 0, 1)
    return out.reshape((b, sq) + out.shape[3:])


def diff_attention(q, k, v, lam, lam_init):
    scale = DA_DH ** -0.5

    def block(qb):
        s = jnp.einsum('bqhcd,bkhcd->bhcqk', qb, k).astype(jnp.float32) * scale
        p = jax.nn.softmax(s, axis=-1)
        p = p[:, :, 0] - lam * p[:, :, 1]
        return jnp.einsum('bhqk,bkhd->bqhd', p.astype(v.dtype), v)

    o = sweep_query_blocks(block, q)
    return rmsnorm(o) * (1.0 - lam_init)


def gqa_attention(q, k, v):
    b, sq, hq, d = q.shape
    hkv = k.shape[2]
    qg = q.reshape(b, sq, hkv, hq // hkv, d)
    scale = d ** -0.5

    def block(qb):
        s = jnp.einsum('bqngd,bsnd->bngqs', qb, k).astype(jnp.float32) * scale
        p = jax.nn.softmax(s, axis=-1).astype(v.dtype)
        return jnp.einsum('bngqs,bsnd->bqngd', p, v)

    return sweep_query_blocks(block, qg).reshape(b, sq, hq, d)


def neighbourhood_attention(q, k, v, k_ctx, v_ctx, rpb):
    b, n, h, d = q.shape
    rows = n // GRID_W
    kr = min(NA_ROWS, rows)
    scale = d ** -0.5
    qg = q.reshape(b, rows, GRID_W, h, d)
    kg = k.reshape(b, rows, GRID_W, h, d)
    vg = v.reshape(b, rows, GRID_W, h, d)
    w_idx = np.arange(GRID_W)
    col_start = np.clip(w_idx - NA_COLS // 2, 0, GRID_W - NA_COLS)
    col_idx = col_start[:, None] + np.arange(NA_COLS)[None, :]
    rpb_cols = rpb[:, :, col_idx - w_idx[:, None] + NA_COLS - 1]
    n_loc = kr * NA_COLS

    def row_block(args):
        r, qr = args
        start = jnp.clip(r - kr // 2, 0, rows - kr)
        kw = lax.dynamic_slice_in_dim(kg, start, kr, axis=1)[:, :, col_idx]
        vw = lax.dynamic_slice_in_dim(vg, start, kr, axis=1)[:, :, col_idx]
        s_loc = jnp.einsum('bwhd,biwjhd->bhwij', qr, kw).astype(jnp.float32) * scale
        row_off = start + jnp.arange(kr) - r + NA_ROWS - 1
        bias = jnp.transpose(rpb_cols[:, row_off], (0, 2, 1, 3)).astype(jnp.float32)
        s_loc = (s_loc + bias[None]).reshape(b, h, GRID_W, n_loc)
        s_ctx = jnp.einsum('bwhd,bphd->bhwp', qr, k_ctx).astype(jnp.float32) * scale
        p = jax.nn.softmax(jnp.concatenate([s_loc, s_ctx], axis=-1), axis=-1).astype(v.dtype)
        p_loc = p[..., :n_loc].reshape(b, h, GRID_W, kr, NA_COLS)
        return (jnp.einsum('bhwij,biwjhd->bwhd', p_loc, vw)
                + jnp.einsum('bhwp,bphd->bwhd', p[..., n_loc:], v_ctx))

    o = lax.map(row_block, (jnp.arange(rows), jnp.moveaxis(qg, 1, 0)))
    return jnp.moveaxis(o, 0, 1).reshape(b, n, h, d)


def centred_conv(x, w, bias):
    s = x.shape[1]
    xp = jnp.pad(x, ((0, 0), (CONV_LEFT, CONV_W - 1 - CONV_LEFT), (0, 0)))
    return bias + sum(xp[:, j:j + s] * w[j] for j in range(CONV_W))


def rglru_coeffs(x, gate_w, gate_b, lam_param):
    b, s, c = x.shape
    xb = x.reshape(b, s, LRU_BLOCKS, LRU_BS)
    gates = jnp.einsum('bsnc,gncd->gbsnd', xb, gate_w).reshape(2, b, s, c) + gate_b[:, None, None, :]
    r, i = jax.nn.sigmoid(gates.astype(jnp.float32))
    log_a = -LRU_C * r * jax.nn.softplus(-lam_param.astype(jnp.float32))
    a = jnp.exp(log_a)
    mult = jnp.sqrt(-jnp.expm1(2.0 * log_a))
    return a, mult * i * x.astype(jnp.float32)


def linear_scan(a, bx, h0, reverse):
    if reverse:
        a, bx = jnp.flip(a, 1), jnp.flip(bx, 1)
    bx = bx.at[:, 0].add(a[:, 0] * h0)

    def combine(e1, e2):
        a1, b1 = e1
        a2, b2 = e2
        return a1 * a2, a2 * b1 + b2

    _, h = lax.associative_scan(combine, (a, bx), axis=1)
    final = h[:, -1]
    if reverse:
        h = jnp.flip(h, 1)
    return h, final


def rglru_bidir(xb, gb, conv_w, conv_b, gate_w, gate_b, lam_param, h0):
    xc = centred_conv(xb, conv_w, conv_b)
    a_f, b_f = rglru_coeffs(xc, gate_w[0], gate_b[0], lam_param[0])
    h_f, fin_f = linear_scan(a_f, b_f, h0[:, 0].astype(jnp.float32), False)
    a_b, b_b = rglru_coeffs(xc, gate_w[1], gate_b[1], lam_param[1])
    h_b, fin_b = linear_scan(a_b, b_b, h0[:, 1].astype(jnp.float32), True)
    y = (h_f + h_b).astype(xb.dtype) * jax.nn.gelu(gb)
    return y, jnp.stack([fin_f, fin_b], axis=1).astype(xb.dtype)


def split_in(u):
    return jnp.split(u, np.cumsum(IN_SPLITS)[:-1].tolist(), axis=-1)


def setup_inputs(seed: int = 0) -> dict:
    key = jax.random.key(seed)
    ks = jax.random.split(key, 24)

    def nrm(k, shape, scale):
        return scale * jax.random.normal(k, shape, jnp.float32)

    u = jax.random.uniform(ks[17], (DEPTH, 2, LRU_WIDTH), jnp.float32, 0.9, 0.999)
    s = u ** (1.0 / LRU_C)
    lru_lambda = jnp.log(s) - jnp.log1p(-s)
    return {
        "x_prompt": nrm(ks[0], (BATCH, SEQ, D_MODEL), 1.0),
        "x_sample": nrm(ks[1], (DEC_BATCH, DEC_SEQ, D_MODEL), 1.0),
        "cache_diff_kv": nrm(ks[2], (DEC_BATCH, DEPTH, 2, PAST_LEN, DA_HEADS, HEAD_DIM), 1.0),
        "state_rglru": nrm(ks[3], (DEC_BATCH, DEPTH, 2, LRU_WIDTH), 0.5),
        "cache_na_kv": nrm(ks[4], (DEC_BATCH, DEPTH, 2, PAST_LEN, NA_HEADS, HEAD_DIM), 1.0),
        "cache_gqa_kv": nrm(ks[5], (DEC_BATCH, DEPTH, 2, PAST_LEN, GQA_KV_HEADS, HEAD_DIM), 1.0),
        "c": nrm(ks[6], (DEC_BATCH, D_MODEL), 1.0),
        "c_ctx": nrm(ks[7], (D_MODEL,), 1.0),
        "w_mod": nrm(ks[8], (DEPTH, D_MODEL, 6 * D_MODEL), 0.5 * D_MODEL ** -0.5),
        "b_mod": nrm(ks[9], (DEPTH, 6 * D_MODEL), 0.01),
        "norm_g": 1.0 + nrm(ks[10], (DEPTH, 4, D_MODEL), 0.05),
        "w_in": nrm(ks[11], (DEPTH, D_MODEL, IN_WIDTH), D_MODEL ** -0.5),
        "diff_lambda_w": nrm(ks[12], (DEPTH, 4, DA_DH), 0.1),
        "lru_conv_w": nrm(ks[13], (DEPTH, CONV_W, LRU_WIDTH), CONV_W ** -0.5),
        "lru_conv_b": nrm(ks[14], (DEPTH, LRU_WIDTH), 0.01),
        "lru_gate_w": nrm(ks[15], (DEPTH, 2, 2, LRU_BLOCKS, LRU_BS, LRU_BS), LRU_BS ** -0.5),
        "lru_gate_b": nrm(ks[16], (DEPTH, 2, 2, LRU_WIDTH), 0.01),
        "lru_lambda": lru_lambda,
        "na_rpb": nrm(ks[18], (DEPTH, NA_HEADS, 2 * NA_ROWS - 1, 2 * NA_COLS - 1), 0.1),
        "gqa_qk_g": 1.0 + nrm(ks[19], (DEPTH, 2, HEAD_DIM), 0.05),
        "w_out": nrm(ks[20], (DEPTH, MIX_WIDTH, D_MODEL), MIX_WIDTH ** -0.5),
        "w_ffn_in": nrm(ks[21], (DEPTH, D_MODEL, 2 * FFN_HIDDEN), D_MODEL ** -0.5),
        "w_ffn_out": nrm(ks[22], (DEPTH, FFN_HIDDEN, D_MODEL), FFN_HIDDEN ** -0.5),
    }


def reference(x_prompt, x_sample, cache_diff_kv, state_rglru, cache_na_kv, cache_gqa_kv,
              c, c_ctx, w_mod, b_mod, norm_g, w_in, diff_lambda_w, lru_conv_w, lru_conv_b,
              lru_gate_w, lru_gate_b, lru_lambda, na_rpb, gqa_qk_g, w_out, w_ffn_in, w_ffn_out):

    def diff_lambda(l):
        lq1, lk1, lq2, lk2 = diff_lambda_w[l].astype(jnp.float32)
        lam_init = 0.8 - 0.6 * math.exp(-0.3 * l)
        lam = jnp.exp(jnp.sum(lq1 * lk1)) - jnp.exp(jnp.sum(lq2 * lk2)) + lam_init
        return lam, lam_init

    def rglru(xb, gb, l, h0):
        return rglru_bidir(xb, gb, lru_conv_w[l], lru_conv_b[l], lru_gate_w[l], lru_gate_b[l],
                           lru_lambda[l], h0)

    def context_mixer(u, l):
        b, s = u.shape[:2]
        aq, ak, av, bx, bg, cq, ck, cv, dq, dk, dv = split_in(u)
        lam, lam_init = diff_lambda(l)
        va = av.reshape(b, s, DA_HEADS, HEAD_DIM)
        o_a = diff_attention(aq.reshape(b, s, DA_HEADS, 2, DA_DH), ak.reshape(b, s, DA_HEADS, 2, DA_DH),
                             va, lam, lam_init)
        kv_a = jnp.stack([ak.reshape(b, s, DA_HEADS, HEAD_DIM), va], axis=1)
        o_b, st_b = rglru(bx, bg, l, jnp.zeros((b, 2, LRU_WIDTH), jnp.float32))
        kc = ck.reshape(b, s, NA_HEADS, HEAD_DIM)
        vc = cv.reshape(b, s, NA_HEADS, HEAD_DIM)
        o_c = gqa_attention(cq.reshape(b, s, NA_HEADS, HEAD_DIM), kc, vc)
        kv_c = jnp.stack([kc, vc], axis=1)
        qd = rmsnorm(dq.reshape(b, s, GQA_Q_HEADS, HEAD_DIM), gqa_qk_g[l, 0])
        kd = rmsnorm(dk.reshape(b, s, GQA_KV_HEADS, HEAD_DIM), gqa_qk_g[l, 1])
        vd = dv.reshape(b, s, GQA_KV_HEADS, HEAD_DIM)
        o_d = gqa_attention(qd, kd, vd)
        kv_d = jnp.stack([kd, vd], axis=1)
        out = jnp.concatenate([o_a.reshape(b, s, -1), o_b, o_c.reshape(b, s, -1), o_d.reshape(b, s, -1)],
                              axis=-1)
        return out, (kv_a, st_b, kv_c, kv_d)

    def latent_mixer(u, l):
        b, n = u.shape[:2]
        p = cache_diff_kv.shape[3]
        aq, ak, av, bx, bg, cq, ck, cv, dq, dk, dv = split_in(u)
        lam, lam_init = diff_lambda(l)
        qa = axial_rope(aq.reshape(b, n, DA_HEADS, 2, DA_DH))
        ka = axial_rope(ak.reshape(b, n, DA_HEADS, 2, DA_DH))
        ka = jnp.concatenate([cache_diff_kv[:, l, 0].reshape(b, p, DA_HEADS, 2, DA_DH), ka], axis=1)
        va = jnp.concatenate([cache_diff_kv[:, l, 1], av.reshape(b, n, DA_HEADS, HEAD_DIM)], axis=1)
        o_a = diff_attention(qa, ka, va, lam, lam_init)
        o_b, _ = rglru(bx, bg, l, state_rglru[:, l])
        o_c = neighbourhood_attention(cq.reshape(b, n, NA_HEADS, HEAD_DIM), ck.reshape(b, n, NA_HEADS, HEAD_DIM),
                                      cv.reshape(b, n, NA_HEADS, HEAD_DIM),
                                      cache_na_kv[:, l, 0], cache_na_kv[:, l, 1], na_rpb[l])
        qd = axial_rope(rmsnorm(dq.reshape(b, n, GQA_Q_HEADS, HEAD_DIM), gqa_qk_g[l, 0]))
        kd = axial_rope(rmsnorm(dk.reshape(b, n, GQA_KV_HEADS, HEAD_DIM), gqa_qk_g[l, 1]))
        kd = jnp.concatenate([cache_gqa_kv[:, l, 0], kd], axis=1)
        vd = jnp.concatenate([cache_gqa_kv[:, l, 1], dv.reshape(b, n, GQA_KV_HEADS, HEAD_DIM)], axis=1)
        o_d = gqa_attention(qd, kd, vd)
        out = jnp.concatenate([o_a.reshape(b, n, -1), o_b, o_c.reshape(b, n, -1), o_d.reshape(b, n, -1)],
                              axis=-1)
        return out, None

    def layer(x, cond, l, mixer):
        mod = (jax.nn.silu(cond) @ w_mod[l] + b_mod[l])[:, None, :]
        sh_m, sc_m, g_m, sh_f, sc_f, g_f = jnp.split(mod, 6, axis=-1)
        h = rmsnorm(x, norm_g[l, 0]) * (1.0 + sc_m) + sh_m
        m, ctx_out = mixer(h @ w_in[l], l)
        x = x + g_m * rmsnorm(m @ w_out[l], norm_g[l, 1])
        h = rmsnorm(x, norm_g[l, 2]) * (1.0 + sc_f) + sh_f
        a, g = jnp.split(h @ w_ffn_in[l], 2, axis=-1)
        x = x + g_f * rmsnorm((jax.nn.silu(g) * a) @ w_ffn_out[l], norm_g[l, 3])
        return x, ctx_out

    y_prompt = x_prompt
    kv_a_list, st_list, kv_c_list, kv_d_list = [], [], [], []
    for l in range(DEPTH):
        y_prompt, (kv_a, st_b, kv_c, kv_d) = layer(y_prompt, c_ctx[None, :], l, context_mixer)
        kv_a_list.append(kv_a)
        st_list.append(st_b)
        kv_c_list.append(kv_c)
        kv_d_list.append(kv_d)
    new_cache_diff_kv = jnp.stack(kv_a_list, axis=1)
    new_state_rglru = jnp.stack(st_list, axis=1)
    new_cache_na_kv = jnp.stack(kv_c_list, axis=1)
    new_cache_gqa_kv = jnp.stack(kv_d_list, axis=1)

    y_sample = x_sample
    for l in range(DEPTH):
        y_sample, _ = layer(y_sample, c, l, latent_mixer)

    return (y_prompt, y_sample, new_cache_diff_kv, new_state_rglru, new_cache_na_kv, new_cache_gqa_kv)
```

```python
import functools
import math

import jax
import jax.numpy as jnp
import numpy as np
from jax import lax
from jax.experimental import pallas as pl
from jax.experimental.pallas import tpu as pltpu

D_MODEL = 2048
BATCH = 16
SEQ = 256
DEPTH = 4
DEC_BATCH = 4
DEC_SEQ = 2048
PAST_LEN = 512
GRID_W = 64
GRID_H = DEC_SEQ // GRID_W
HEAD_DIM = 128
EPS = 1e-6
ROPE_THETA = 10000.0
DA_HEADS = 4
DA_DH = HEAD_DIM // 2
LRU_WIDTH = 512
LRU_BLOCKS = 4
LRU_BS = LRU_WIDTH // LRU_BLOCKS
LRU_C = 8.0
CONV_W = 4
CONV_LEFT = 2
NA_HEADS = 4
NA_ROWS = 8
NA_COLS = 16
GQA_Q_HEADS = 4
GQA_KV_HEADS = 2
IN_WIDTH = 5120
MIX_WIDTH = 2048
FFN_HIDDEN = 5632
N_MOD = 6 * D_MODEL

F32 = jnp.float32
BF16 = jnp.bfloat16

VMEM_LIMIT_BYTES = 56 * 1024 * 1024
LANES = 128

COL_AQ, COL_AK, COL_AV, COL_BX, COL_BG, COL_CQ, COL_CK, COL_CV, COL_DQ, COL_DKV = range(10)
N_COL_BLOCKS = 10
COL_BLOCK = 512

NA_QROWS = 4
NA_WIN_ROWS = NA_QROWS + NA_ROWS
NA_TQ = NA_QROWS * GRID_W
NA_WIN = NA_WIN_ROWS * GRID_W
NEG_BIG = -1e30


def _params(sem):
    return pltpu.CompilerParams(dimension_semantics=sem, vmem_limit_bytes=VMEM_LIMIT_BYTES)


def _rms(x):
    return x * lax.rsqrt(jnp.mean(x * x, axis=-1, keepdims=True) + EPS)


def _dot(a, b):
    return jnp.dot(a, b, preferred_element_type=F32)


def _dot_nt(a, b):
    return lax.dot_general(a, b, (((1,), (1,)), ((), ())), preferred_element_type=F32)


MOD_ROWS = 8
MOD_TN = 1024


def _mod_kernel(cond_ref, w_ref, b_ref, o_ref):
    c = cond_ref[...]
    s = (c * jax.nn.sigmoid(c)).astype(BF16)
    o_ref[...] = _dot(s, w_ref[...].astype(BF16)) + b_ref[...]


def _mod_call(cond, w_mod, b_mod):
    return pl.pallas_call(
        _mod_kernel,
        grid=(DEPTH, N_MOD // MOD_TN),
        in_specs=[
            pl.BlockSpec((MOD_ROWS, D_MODEL), lambda l, n: (0, 0)),
            pl.BlockSpec((None, D_MODEL, MOD_TN), lambda l, n: (l, 0, n)),
            pl.BlockSpec((None, 1, MOD_TN), lambda l, n: (l, 0, n)),
        ],
        out_specs=pl.BlockSpec((None, MOD_ROWS, MOD_TN), lambda l, n: (l, 0, n)),
        out_shape=jax.ShapeDtypeStruct((DEPTH, MOD_ROWS, N_MOD), F32),
        compiler_params=_params(("arbitrary", "arbitrary")),
        name="mod",
    )(cond, w_mod, b_mod.reshape(DEPTH, 1, N_MOD))


IN_TM = 512


def _rope(seg, cos, sin_lo, sin_hi, off):
    return (seg * cos + pltpu.roll(seg, LANES - off, 1) * sin_lo + pltpu.roll(seg, off, 1) * sin_hi)


def _inproj_kernel(*refs, latent):
    if latent:
        (x_ref, mod_ref, g_ref, w_ref, qkg_ref, ra_ref, rd_ref, ub_ref, uf_ref, h_scr) = refs
    else:
        (x_ref, mod_ref, g_ref, w_ref, qkg_ref, ub_ref, uf_ref, h_scr) = refs
    j = pl.program_id(1)

    @pl.when(j == 0)
    def _():
        y = _rms(x_ref[...]) * g_ref[0:1, :]
        h_scr[...] = (y * (1.0 + mod_ref[1:2, :]) + mod_ref[0:1, :]).astype(BF16)

    acc = _dot(h_scr[...], w_ref[...])

    def heads(fn):
        for hh in range(COL_BLOCK // LANES):
            sl = slice(hh * LANES, (hh + 1) * LANES)
            fn(hh, sl, acc[:, sl])

    def rope_a(seg):
        return _rope(seg, ra_ref[0], ra_ref[1], ra_ref[2], DA_DH // 4) if latent else seg

    def rope_d(seg):
        return _rope(seg, rd_ref[0], rd_ref[1], rd_ref[2], HEAD_DIM // 4) if latent else seg

    def store(sl, val, want_f32):
        ub_ref[:, sl] = val.astype(BF16)
        if want_f32:
            uf_ref[:, sl] = val

    @pl.when(j == COL_AQ)
    def _():
        heads(lambda hh, sl, seg: store(sl, rope_a(seg), not latent))

    @pl.when(j == COL_AK)
    def _():
        heads(lambda hh, sl, seg: store(sl, rope_a(seg), not latent))

    plain_f32 = (COL_BX, COL_BG)
    plain = (COL_AV, COL_BX, COL_BG, COL_CQ, COL_CK, COL_CV)
    for col in plain:
        @pl.when(j == col)
        def _(col=col):
            ub_ref[...] = acc.astype(BF16)
            if (not latent) or (col in plain_f32):
                uf_ref[...] = acc

    @pl.when(j == COL_DQ)
    def _():
        heads(lambda hh, sl, seg: store(sl, rope_d(_rms(seg) * qkg_ref[0:1, :]), not latent))

    @pl.when(j == COL_DKV)
    def _():
        def fn(hh, sl, seg):
            if hh < GQA_KV_HEADS:
                seg = rope_d(_rms(seg) * qkg_ref[1:2, :])
            store(sl, seg, not latent)
        heads(fn)


def _inproj_call(x, mod_l, norm_g_l, w_in_l, qk_g_l, rope_tabs, *, latent):
    rows = x.shape[0]
    tm = IN_TM
    per_batch = DEC_SEQ // tm
    if latent:
        midx = lambda i: 1 + i // per_batch
        uf_cols = 2 * COL_BLOCK
        uf_map = lambda i, j: (i, jnp.clip(j - COL_BX, 0, 1))
    else:
        midx = lambda i: 0
        uf_cols = IN_WIDTH
        uf_map = lambda i, j: (i, j)
    in_specs = [
        pl.BlockSpec((tm, D_MODEL), lambda i, j: (i, 0)),
        pl.BlockSpec((None, 6, D_MODEL), lambda i, j: (midx(i), 0, 0)),
        pl.BlockSpec((4, D_MODEL), lambda i, j: (0, 0)),
        pl.BlockSpec((D_MODEL, COL_BLOCK), lambda i, j: (0, j)),
        pl.BlockSpec((2, HEAD_DIM), lambda i, j: (0, 0)),
    ]
    args = [x, mod_l, norm_g_l, w_in_l, qk_g_l]
    if latent:
        tab_spec = pl.BlockSpec((3, tm, LANES), lambda i, j: (0, i % per_batch, 0))
        in_specs += [tab_spec, tab_spec]
        args += [rope_tabs[0], rope_tabs[1]]
    return pl.pallas_call(
        functools.partial(_inproj_kernel, latent=latent),
        grid=(rows // tm, N_COL_BLOCKS),
        in_specs=in_specs,
        out_specs=[
            pl.BlockSpec((tm, COL_BLOCK), lambda i, j: (i, j)),
            pl.BlockSpec((tm, COL_BLOCK), uf_map),
        ],
        out_shape=[
            jax.ShapeDtypeStruct((rows, IN_WIDTH), BF16),
            jax.ShapeDtypeStruct((rows, uf_cols), F32),
        ],
        scratch_shapes=[pltpu.VMEM((tm, D_MODEL), BF16)],
        compiler_params=_params(("arbitrary", "arbitrary")),
        name="inproj_lat" if latent else "inproj_ctx",
    )(*args)


def _rope_tables():
    t = jnp.arange(DEC_SEQ)
    row = (t // GRID_W).astype(F32)[:, None]
    col = (t % GRID_W).astype(F32)[:, None]
    lane = np.arange(LANES)

    def table(d):
        m = d // 2
        freqs = ROPE_THETA ** (-jnp.arange(0, m, 2, dtype=F32) / m)
        within = lane % m
        k = within % (m // 2)
        hi = (within >= m // 2)[None, :]
        use_col = ((lane % d) >= m)[None, :]
        ang = jnp.where(use_col, col, row) * freqs[k][None, :]
        cos, sin = jnp.cos(ang), jnp.sin(ang)
        return jnp.stack([cos, jnp.where(hi, 0.0, -sin), jnp.where(hi, sin, 0.0)])

    return table(DA_DH), table(HEAD_DIM)


def _softmax_pieces(pieces):
    m = pieces[0].max(axis=-1, keepdims=True)
    for s in pieces[1:]:
        m = jnp.maximum(m, s.max(axis=-1, keepdims=True))
    es = [jnp.exp(s - m) for s in pieces]
    l = es[0].sum(axis=-1, keepdims=True)
    for e in es[1:]:
        l = l + e.sum(axis=-1, keepdims=True)
    return es, 1.0 / l


ATT_TQ = 256


def _diff_kernel(*refs, lam_init, latent):
    if latent:
        lamw_ref, q_ref, k_ref, v_ref, kc_ref, vc_ref, o_ref = refs
    else:
        lamw_ref, q_ref, k_ref, v_ref, o_ref = refs
    lw = lamw_ref[...]
    lam = (jnp.exp(jnp.sum(lw[0:1] * lw[1:2], axis=-1, keepdims=True))
           - jnp.exp(jnp.sum(lw[2:3] * lw[3:4], axis=-1, keepdims=True)) + lam_init)
    q = q_ref[...]
    lane = lax.broadcasted_iota(jnp.int32, q.shape, 1)
    zero = jnp.zeros_like(q)
    q1 = jnp.where(lane < DA_DH, q, zero)
    q2 = jnp.where(lane >= DA_DH, q, zero)
    ks = [k_ref[...]]
    vs = [v_ref[...]]
    if latent:
        ks = [kc_ref[...].astype(BF16)] + ks
        vs = [vc_ref[...].astype(BF16)] + vs
    scale = DA_DH ** -0.5
    e1, r1 = _softmax_pieces([_dot_nt(q1, k) * scale for k in ks])
    e2, r2 = _softmax_pieces([_dot_nt(q2, k) * scale for k in ks])
    r2 = r2 * lam
    o = None
    for a, b, v in zip(e1, e2, vs):
        p = (a * r1 - b * r2).astype(BF16)
        c = _dot(p, v)
        o = c if o is None else o + c
    o_ref[...] = (_rms(o) * (1.0 - lam_init)).astype(BF16)


def _diff_call(ub, lam_w, cache, layer, *, latent):
    s = DEC_SEQ if latent else SEQ
    nb = DEC_BATCH if latent else BATCH
    tq = ATT_TQ
    nq = s // tq
    lam_init = 0.8 - 0.6 * math.exp(-0.3 * layer)
    in_specs = [
        pl.BlockSpec((4, DA_DH), lambda b, h, i: (0, 0)),
        pl.BlockSpec((tq, LANES), lambda b, h, i: (b * nq + i, COL_AQ * 4 + h)),
        pl.BlockSpec((s, LANES), lambda b, h, i: (b, COL_AK * 4 + h)),
        pl.BlockSpec((s, LANES), lambda b, h, i: (b, COL_AV * 4 + h)),
    ]
    args = [lam_w, ub, ub, ub]
    if latent:
        in_specs += [
            pl.BlockSpec((None, None, None, PAST_LEN, LANES), lambda b, h, i: (b, layer, 0, 0, h)),
            pl.BlockSpec((None, None, None, PAST_LEN, LANES), lambda b, h, i: (b, layer, 1, 0, h)),
        ]
        args += [cache, cache]
    return pl.pallas_call(
        functools.partial(_diff_kernel, lam_init=lam_init, latent=latent),
        grid=(nb, DA_HEADS, nq),
        in_specs=in_specs,
        out_specs=pl.BlockSpec((tq, LANES), lambda b, h, i: (b * nq + i, h)),
        out_shape=jax.ShapeDtypeStruct((nb * s, DA_HEADS * HEAD_DIM), BF16),
        compiler_params=_params(("arbitrary", "arbitrary", "arbitrary")),
        name="diff_lat" if latent else "diff_ctx",
    )(*args)


def _gqa_kernel(*refs, group, latent):
    if latent:
        q_ref, k_ref, v_ref, kc_ref, vc_ref, o_ref = refs
    else:
        q_ref, k_ref, v_ref, o_ref = refs
    ks = [k_ref[...]]
    vs = [v_ref[...]]
    if latent:
        ks = [kc_ref[...].astype(BF16)] + ks
        vs = [vc_ref[...].astype(BF16)] + vs
    scale = HEAD_DIM ** -0.5
    for g in range(group):
        sl = slice(g * LANES, (g + 1) * LANES)
        q = q_ref[:, sl]
        es, r = _softmax_pieces([_dot_nt(q, k) * scale for k in ks])
        o = None
        for e, v in zip(es, vs):
            c = _dot((e * r).astype(BF16), v)
            o = c if o is None else o + c
        o_ref[:, sl] = o.astype(BF16)


def _gqa_call(ub, cache, layer, *, col_q, col_k, col_v, kv_heads, group, latent, name):
    s = DEC_SEQ if latent else SEQ
    nb = DEC_BATCH if latent else BATCH
    tq = ATT_TQ
    nq = s // tq
    qw = group * LANES
    in_specs = [
        pl.BlockSpec((tq, qw), lambda b, n, i: (b * nq + i, col_q // group + n)),
        pl.BlockSpec((s, LANES), lambda b, n, i: (b, col_k + n)),
        pl.BlockSpec((s, LANES), lambda b, n, i: (b, col_v + n)),
    ]
    args = [ub, ub, ub]
    if latent:
        in_specs += [
            pl.BlockSpec((None, None, None, PAST_LEN, LANES), lambda b, n, i: (b, layer, 0, 0, n)),
            pl.BlockSpec((None, None, None, PAST_LEN, LANES), lambda b, n, i: (b, layer, 1, 0, n)),
        ]
        args += [cache, cache]
    return pl.pallas_call(
        functools.partial(_gqa_kernel, group=group, latent=latent),
        grid=(nb, kv_heads, nq),
        in_specs=in_specs,
        out_specs=pl.BlockSpec((tq, qw), lambda b, n, i: (b * nq + i, n)),
        out_shape=jax.ShapeDtypeStruct((nb * s, kv_heads * qw), BF16),
        compiler_params=_params(("arbitrary", "arbitrary", "arbitrary")),
        name=name,
    )(*args)


def _na_bias_tables(rpb_l):
    qa = np.arange(NA_TQ) // GRID_W
    qc = np.arange(NA_TQ) % GRID_W
    kj = np.arange(NA_WIN) // GRID_W
    kc = np.arange(NA_WIN) % GRID_W
    cstart = np.clip(qc - NA_COLS // 2, 0, GRID_W - NA_COLS)
    col_ok = (kc[None, :] >= cstart[:, None]) & (kc[None, :] < cstart[:, None] + NA_COLS)
    dc = np.clip(kc[None, :] - qc[:, None] + NA_COLS - 1, 0, 2 * NA_COLS - 2)
    tabs = []
    for r0 in (0, NA_QROWS, GRID_H - NA_QROWS):
        qr = r0 + qa
        kr = int(np.clip(r0 - NA_ROWS // 2, 0, GRID_H - NA_WIN_ROWS)) + kj
        rstart = np.clip(qr - NA_ROWS // 2, 0, GRID_H - NA_ROWS)
        row_ok = (kr[None, :] >= rstart[:, None]) & (kr[None, :] < rstart[:, None] + NA_ROWS)
        dr = np.clip(kr[None, :] - qr[:, None] + NA_ROWS - 1, 0, 2 * NA_ROWS - 2)
        bias = rpb_l[:, dr, dc].astype(F32)
        tabs.append(jnp.where((row_ok & col_ok)[None], bias, NEG_BIG))
    return jnp.stack(tabs, axis=1)


def _na_win_start(rb):
    return jnp.clip(rb * NA_QROWS - NA_ROWS // 2, 0, GRID_H - NA_WIN_ROWS)


def _na_kernel(q_ref, k_ref, v_ref, kc_ref, vc_ref, bias_ref, o_ref):
    rb = pl.program_id(2)
    w0 = pl.multiple_of(_na_win_start(rb) * GRID_W, GRID_W)
    q = q_ref[...]
    kw = k_ref[pl.ds(w0, NA_WIN), :]
    vw = v_ref[pl.ds(w0, NA_WIN), :]
    scale = HEAD_DIM ** -0.5
    s_loc = _dot_nt(q, kw) * scale + bias_ref[...]
    s_ctx = _dot_nt(q, kc_ref[...].astype(BF16)) * scale
    (e_loc, e_ctx), r = _softmax_pieces([s_loc, s_ctx])
    o = _dot((e_loc * r).astype(BF16), vw) + _dot((e_ctx * r).astype(BF16), vc_ref[...].astype(BF16))
    o_ref[...] = o.astype(BF16)


def _na_call(ub, cache, bias_tabs, layer):
    nrb = GRID_H // NA_QROWS
    case = lambda rb: jnp.where(rb == 0, 0, jnp.where(rb == nrb - 1, 2, 1))
    return pl.pallas_call(
        _na_kernel,
        grid=(DEC_BATCH, NA_HEADS, nrb),
        in_specs=[
            pl.BlockSpec((NA_TQ, LANES), lambda b, h, r: (b * nrb + r, COL_CQ * 4 + h)),
            pl.BlockSpec((DEC_SEQ, LANES), lambda b, h, r: (b, COL_CK * 4 + h)),
            pl.BlockSpec((DEC_SEQ, LANES), lambda b, h, r: (b, COL_CV * 4 + h)),
            pl.BlockSpec((None, None, None, PAST_LEN, LANES), lambda b, h, r: (b, layer, 0, 0, h)),
            pl.BlockSpec((None, None, None, PAST_LEN, LANES), lambda b, h, r: (b, layer, 1, 0, h)),
            pl.BlockSpec((None, None, NA_TQ, NA_WIN), lambda b, h, r: (h, case(r), 0, 0)),
        ],
        out_specs=pl.BlockSpec((NA_TQ, LANES), lambda b, h, r: (b * nrb + r, h)),
        out_shape=jax.ShapeDtypeStruct((DEC_BATCH * DEC_SEQ, NA_HEADS * HEAD_DIM), BF16),
        compiler_params=_params(("arbitrary", "arbitrary", "arbitrary")),
        name="na_lat",
    )(ub, ub, ub, cache, cache, bias_tabs)


LRU_TC = 256
SUB = 8
PAD = 8


def _gelu_tanh(x):
    return x * (0.5 * (1.0 + jnp.tanh(math.sqrt(2.0 / math.pi) * (x + 0.044715 * (x * x * x)))))


def _lru_kernel(x_ref, g_ref, cw_ref, cb_ref, gw_ref, gb_ref, lam_ref, h0_ref,
                y_ref, st_ref, xp_scr, xc_scr, hf_scr, a_scr, b_scr, *, seq):
    n_chunks = seq // LRU_TC
    zeros_pad = jnp.zeros((PAD, LRU_WIDTH), F32)
    xp_scr[0:PAD, :] = zeros_pad
    xp_scr[PAD + seq:2 * PAD + seq, :] = zeros_pad
    xp_scr[PAD:PAD + seq, :] = x_ref[...]

    rowid = lax.broadcasted_iota(jnp.int32, (SUB, LRU_WIDTH), 0)

    def coeffs(xc, d):
        xcb = xc.astype(BF16)
        pre = []
        for gi in range(2):
            parts = [_dot(xcb[:, n * LRU_BS:(n + 1) * LRU_BS], gw_ref[d, gi, n].astype(BF16))
                     for n in range(LRU_BLOCKS)]
            pre.append(jnp.concatenate(parts, axis=-1) + gb_ref[d, gi:gi + 1, :])
        r = jax.nn.sigmoid(pre[0])
        i = jax.nn.sigmoid(pre[1])
        log_a = -LRU_C * r * jax.nn.softplus(-lam_ref[d:d + 1, :])
        a = jnp.exp(log_a)
        mult = jnp.sqrt(-jnp.tanh(log_a) * (a * a + 1.0))
        return a, mult * i * xc

    def scan_tile(a, b, carry, reverse):
        for d in (1, 2, 4):
            if reverse:
                keep = rowid < SUB - d
                sh = SUB - d
            else:
                keep = rowid >= d
                sh = d
            a_s = jnp.where(keep, pltpu.roll(a, sh, 0), 1.0)
            b_s = jnp.where(keep, pltpu.roll(b, sh, 0), 0.0)
            b = a * b_s + b
            a = a * a_s
        return a * carry + b

    carry = h0_ref[0:1, :]
    for c in range(n_chunks):
        r0 = c * LRU_TC
        xc = cb_ref[...] + sum(
            xp_scr[PAD + r0 + j - CONV_LEFT:PAD + r0 + j - CONV_LEFT + LRU_TC, :] * cw_ref[j:j + 1, :]
            for j in range(CONV_W))
        xc_scr[r0:r0 + LRU_TC, :] = xc
        a, b = coeffs(xc, 0)
        a_scr[...] = a
        b_scr[...] = b

        def body(t, carry, r0=r0):
            r = pl.multiple_of(t * SUB, SUB)
            h = scan_tile(a_scr[pl.ds(r, SUB), :], b_scr[pl.ds(r, SUB), :], carry, False)
            hf_scr[pl.ds(r0 + r, SUB), :] = h
            return h[SUB - 1:SUB, :]

        carry = lax.fori_loop(0, LRU_TC // SUB, body, carry)
    st_ref[0:1, :] = carry

    carry = h0_ref[1:2, :]
    for c in reversed(range(n_chunks)):
        r0 = c * LRU_TC
        a, b = coeffs(xc_scr[r0:r0 + LRU_TC, :], 1)
        a_scr[...] = a
        b_scr[...] = b

        def body(tt, carry):
            r = pl.multiple_of((LRU_TC // SUB - 1 - tt) * SUB, SUB)
            h = scan_tile(a_scr[pl.ds(r, SUB), :], b_scr[pl.ds(r, SUB), :], carry, True)
            b_scr[pl.ds(r, SUB), :] = h
            return h[0:1, :]

        carry = lax.fori_loop(0, LRU_TC // SUB, body, carry)
        y = (hf_scr[r0:r0 + LRU_TC, :] + b_scr[...]) * _gelu_tanh(g_ref[r0:r0 + LRU_TC, :])
        y_ref[r0:r0 + LRU_TC, :] = y.astype(BF16)
    st_ref[1:2, :] = carry


def _lru_call(uf, col_x, col_g, conv_w, conv_b, gate_w, gate_b, lam, h0, *, seq, nb, name):
    const = lambda nd: (lambda b: (0,) * nd)
    return pl.pallas_call(
        functools.partial(_lru_kernel, seq=seq),
        grid=(nb,),
        in_specs=[
            pl.BlockSpec((seq, LRU_WIDTH), lambda b: (b, col_x)),
            pl.BlockSpec((seq, LRU_WIDTH), lambda b: (b, col_g)),
            pl.BlockSpec((CONV_W, LRU_WIDTH), const(2)),
            pl.BlockSpec((1, LRU_WIDTH), const(2)),
            pl.BlockSpec((2, 2, LRU_BLOCKS, LRU_BS, LRU_BS), const(5)),
            pl.BlockSpec((2, 2, LRU_WIDTH), const(3)),
            pl.BlockSpec((2, LRU_WIDTH), const(2)),
            pl.BlockSpec((None, 2, LRU_WIDTH), lambda b: (b, 0, 0)),
        ],
        out_specs=[
            pl.BlockSpec((seq, LRU_WIDTH), lambda b: (b, 0)),
            pl.BlockSpec((None, 2, LRU_WIDTH), lambda b: (b, 0, 0)),
        ],
        out_shape=[
            jax.ShapeDtypeStruct((nb * seq, LRU_WIDTH), BF16),
            jax.ShapeDtypeStruct((nb, 2, LRU_WIDTH), F32),
        ],
        scratch_shapes=[
            pltpu.VMEM((seq + 2 * PAD, LRU_WIDTH), F32),
            pltpu.VMEM((seq, LRU_WIDTH), F32),
            pltpu.VMEM((seq, LRU_WIDTH), F32),
            pltpu.VMEM((LRU_TC, LRU_WIDTH), F32),
            pltpu.VMEM((LRU_TC, LRU_WIDTH), F32),
        ],
        compiler_params=_params(("arbitrary",)),
        name=name,
    )(uf, uf, conv_w, conv_b.reshape(1, LRU_WIDTH), gate_w, gate_b, lam, h0)


OUT_TM = 256


def _outproj_kernel(x_ref, ma_ref, mb_ref, mc_ref, md_ref, mod_ref, g_ref, w_ref, x1_ref, h2_ref):
    y = None
    for n, m_ref in enumerate((ma_ref, mb_ref, mc_ref, md_ref)):
        c = _dot(m_ref[...], w_ref[n * COL_BLOCK:(n + 1) * COL_BLOCK, :])
        y = c if y is None else y + c
    x1 = x_ref[...] + mod_ref[2:3, :] * (_rms(y) * g_ref[1:2, :])
    x1_ref[...] = x1
    h2 = _rms(x1) * g_ref[2:3, :]
    h2_ref[...] = (h2 * (1.0 + mod_ref[4:5, :]) + mod_ref[3:4, :]).astype(BF16)


def _outproj_call(x, parts, mod_l, norm_g_l, w_out_l, *, latent):
    rows = x.shape[0]
    tm = OUT_TM
    midx = (lambda i: 1 + i // (DEC_SEQ // tm)) if latent else (lambda i: 0)
    part_spec = pl.BlockSpec((tm, COL_BLOCK), lambda i: (i, 0))
    return pl.pallas_call(
        _outproj_kernel,
        grid=(rows // tm,),
        in_specs=[
            pl.BlockSpec((tm, D_MODEL), lambda i: (i, 0)),
            part_spec, part_spec, part_spec, part_spec,
            pl.BlockSpec((None, 6, D_MODEL), lambda i: (midx(i), 0, 0)),
            pl.BlockSpec((4, D_MODEL), lambda i: (0, 0)),
            pl.BlockSpec((MIX_WIDTH, D_MODEL), lambda i: (0, 0)),
        ],
        out_specs=[
            pl.BlockSpec((tm, D_MODEL), lambda i: (i, 0)),
            pl.BlockSpec((tm, D_MODEL), lambda i: (i, 0)),
        ],
        out_shape=[
            jax.ShapeDtypeStruct((rows, D_MODEL), F32),
            jax.ShapeDtypeStruct((rows, D_MODEL), BF16),
        ],
        compiler_params=_params(("arbitrary",)),
        name="outproj_lat" if latent else "outproj_ctx",
    )(x, *parts, mod_l, norm_g_l, w_out_l)


FFN_TM = 512
FFN_TH = 512


def _ffn_kernel(h_ref, x1_ref, mod_ref, g_ref, wa_ref, wg_ref, wo_ref, o_ref, acc_scr):
    j = pl.program_id(1)
    h = h_ref[...]
    a = _dot(h, wa_ref[...])
    g = _dot(h, wg_ref[...])
    t = ((g * jax.nn.sigmoid(g)) * a).astype(BF16)
    c = _dot(t, wo_ref[...])

    @pl.when(j == 0)
    def _():
        acc_scr[...] = c

    @pl.when(j > 0)
    def _():
        acc_scr[...] += c

    @pl.when(j == pl.num_programs(1) - 1)
    def _():
        o_ref[...] = x1_ref[...] + mod_ref[5:6, :] * (_rms(acc_scr[...]) * g_ref[3:4, :])


def _ffn_call(h2, x1, mod_l, norm_g_l, w_ffn_in_l, w_ffn_out_l, *, latent):
    rows = x1.shape[0]
    tm, th = FFN_TM, FFN_TH
    nh = FFN_HIDDEN // th
    midx = (lambda i: 1 + i // (DEC_SEQ // tm)) if latent else (lambda i: 0)
    return pl.pallas_call(
        _ffn_kernel,
        grid=(rows // tm, nh),
        in_specs=[
            pl.BlockSpec((tm, D_MODEL), lambda i, j: (i, 0)),
            pl.BlockSpec((tm, D_MODEL), lambda i, j: (i, 0)),
            pl.BlockSpec((None, 6, D_MODEL), lambda i, j: (midx(i), 0, 0)),
            pl.BlockSpec((4, D_MODEL), lambda i, j: (0, 0)),
            pl.BlockSpec((D_MODEL, th), lambda i, j: (0, j)),
            pl.BlockSpec((D_MODEL, th), lambda i, j: (0, nh + j)),
            pl.BlockSpec((th, D_MODEL), lambda i, j: (j, 0)),
        ],
        out_specs=pl.BlockSpec((tm, D_MODEL), lambda i, j: (i, 0)),
        out_shape=jax.ShapeDtypeStruct((rows, D_MODEL), F32),
        scratch_shapes=[pltpu.VMEM((tm, D_MODEL), F32)],
        compiler_params=_params(("arbitrary", "arbitrary")),
        name="ffn_lat" if latent else "ffn_ctx",
    )(h2, x1, mod_l, norm_g_l, w_ffn_in_l, w_ffn_in_l, w_ffn_out_l)


def kernel(x_prompt, x_sample, cache_diff_kv, state_rglru, cache_na_kv, cache_gqa_kv, c, c_ctx, w_mod, b_mod,
           norm_g, w_in, diff_lambda_w, lru_conv_w, lru_conv_b, lru_gate_w, lru_gate_b, lru_lambda, na_rpb,
           gqa_qk_g, w_out, w_ffn_in, w_ffn_out):
    cond = jnp.concatenate([c_ctx[None, :], c, jnp.zeros((MOD_ROWS - 1 - DEC_BATCH, D_MODEL), F32)], axis=0)
    mod = _mod_call(cond, w_mod, b_mod).reshape(DEPTH, MOD_ROWS, 6, D_MODEL)

    w_in_b = w_in.astype(BF16)
    w_out_b = w_out.astype(BF16)
    w_ffn_in_b = w_ffn_in.astype(BF16)
    w_ffn_out_b = w_ffn_out.astype(BF16)
    rope_tabs = _rope_tables()

    cache_a = cache_diff_kv.reshape(DEC_BATCH, DEPTH, 2, PAST_LEN, DA_HEADS * HEAD_DIM)
    cache_c = cache_na_kv.reshape(DEC_BATCH, DEPTH, 2, PAST_LEN, NA_HEADS * HEAD_DIM)
    cache_d = cache_gqa_kv.reshape(DEC_BATCH, DEPTH, 2, PAST_LEN, GQA_KV_HEADS * HEAD_DIM)
    zero_state = jnp.zeros((BATCH, 2, LRU_WIDTH), F32)

    def layer(x, l, latent):
        ub, uf = _inproj_call(x, mod[l], norm_g[l], w_in_b[l], gqa_qk_g[l], rope_tabs, latent=latent)
        m_a = _diff_call(ub, diff_lambda_w[l], cache_a, l, latent=latent)
        if latent:
            m_b, st = _lru_call(uf, 0, 1, lru_conv_w[l], lru_conv_b[l], lru_gate_w[l], lru_gate_b[l],
                                lru_lambda[l], state_rglru[:, l], seq=DEC_SEQ, nb=DEC_BATCH, name="lru_lat")
            m_c = _na_call(ub, cache_c, _na_bias_tables(na_rpb[l]), l)
        else:
            m_b, st = _lru_call(uf, COL_BX, COL_BG, lru_conv_w[l], lru_conv_b[l], lru_gate_w[l], lru_gate_b[l],
                                lru_lambda[l], zero_state, seq=SEQ, nb=BATCH, name="lru_ctx")
            m_c = _gqa_call(ub, None, l, col_q=COL_CQ * 4, col_k=COL_CK * 4, col_v=COL_CV * 4,
                            kv_heads=NA_HEADS, group=1, latent=False, name="attn_c_ctx")
        m_d = _gqa_call(ub, cache_d, l, col_q=COL_DQ * 4, col_k=COL_DKV * 4, col_v=COL_DKV * 4 + GQA_KV_HEADS,
                        kv_heads=GQA_KV_HEADS, group=GQA_Q_HEADS // GQA_KV_HEADS, latent=latent,
                        name="attn_d_lat" if latent else "attn_d_ctx")
        x1, h2 = _outproj_call(x, (m_a, m_b, m_c, m_d), mod[l], norm_g[l], w_out_b[l], latent=latent)
        x_new = _ffn_call(h2, x1, mod[l], norm_g[l], w_ffn_in_b[l], w_ffn_out_b[l], latent=latent)
        return x_new, uf, st

    y = x_prompt.reshape(BATCH * SEQ, D_MODEL)
    kv_a, kv_c, kv_d, states = [], [], [], []
    for l in range(DEPTH):
        y, uf, st = layer(y, l, False)
        u4 = uf.reshape(BATCH, SEQ, N_COL_BLOCKS, COL_BLOCK)
        kv_a.append(jnp.stack([u4[:, :, COL_AK], u4[:, :, COL_AV]], axis=1))
        kv_c.append(jnp.stack([u4[:, :, COL_CK], u4[:, :, COL_CV]], axis=1))
        dkv = u4[:, :, COL_DKV]
        half = GQA_KV_HEADS * HEAD_DIM
        kv_d.append(jnp.stack([dkv[..., :half], dkv[..., half:]], axis=1))
        states.append(st)
    y_prompt = y.reshape(BATCH, SEQ, D_MODEL)
    new_cache_diff_kv = jnp.stack(kv_a, axis=1).reshape(BATCH, DEPTH, 2, SEQ, DA_HEADS, HEAD_DIM)
    new_state_rglru = jnp.stack(states, axis=1)
    new_cache_na_kv = jnp.stack(kv_c, axis=1).reshape(BATCH, DEPTH, 2, SEQ, NA_HEADS, HEAD_DIM)
    new_cache_gqa_kv = jnp.stack(kv_d, axis=1).reshape(BATCH, DEPTH, 2, SEQ, GQA_KV_HEADS, HEAD_DIM)

    z = x_sample.reshape(DEC_BATCH * DEC_SEQ, D_MODEL)
    for l in range(DEPTH):
        z, _, _ = layer(z, l, True)
    y_sample = z.reshape(DEC_BATCH, DEC_SEQ, D_MODEL)

    return (y_prompt, y_sample, new_cache_diff_kv, new_state_rglru, new_cache_na_kv, new_cache_gqa_kv)
```

```python
import functools
import math

import jax
import jax.numpy as jnp
import numpy as np
from jax import lax
from jax.experimental import pallas as pl
from jax.experimental.pallas import tpu as pltpu

D_MODEL = 2048
BATCH = 16
SEQ = 256
DEPTH = 4
DEC_BATCH = 4
DEC_SEQ = 2048
PAST_LEN = 512
GRID_W = 64
GRID_H = DEC_SEQ // GRID_W
HEAD_DIM = 128
EPS = 1e-6
ROPE_THETA = 10000.0
DA_HEADS = 4
DA_DH = HEAD_DIM // 2
LRU_WIDTH = 512
LRU_BLOCKS = 4
LRU_BS = LRU_WIDTH // LRU_BLOCKS
LRU_C = 8.0
CONV_W = 4
CONV_LEFT = 2
NA_HEADS = 4
NA_ROWS = 8
NA_COLS = 16
GQA_Q_HEADS = 4
GQA_KV_HEADS = 2
IN_WIDTH = 5120
MIX_WIDTH = 2048
FFN_HIDDEN = 5632
N_MOD = 6 * D_MODEL

F32 = jnp.float32
BF16 = jnp.bfloat16

VMEM_LIMIT_BYTES = 56 * 1024 * 1024
LANES = 128
LOG2E = math.log2(math.e)

COL_AQ, COL_AK, COL_AV, COL_BX, COL_BG, COL_CQ, COL_CK, COL_CV, COL_DQ, COL_DKV = range(10)
N_COL_BLOCKS = 10
COL_BLOCK = 512
HEADS_PER_BLOCK = COL_BLOCK // LANES

NA_QROWS = 4
NA_WIN_ROWS = NA_QROWS + NA_ROWS
NA_TQ = NA_QROWS * GRID_W
NA_WIN = NA_WIN_ROWS * GRID_W
NEG_BIG = -1e30

RESIDENT = pl.Buffered(1)


def _params(sem):
    return pltpu.CompilerParams(dimension_semantics=sem, vmem_limit_bytes=VMEM_LIMIT_BYTES)


def _rms(x):
    return x * lax.rsqrt(jnp.mean(x * x, axis=-1, keepdims=True) + EPS)


def _dot(a, b):
    return jnp.dot(a, b, preferred_element_type=F32)


def _dot_nt(a, b):
    return lax.dot_general(a, b, (((1,), (1,)), ((), ())), preferred_element_type=F32)


MOD_ROWS = 8
MOD_TN = 1024


def _mod_kernel(cond_ref, w_ref, b_ref, o_ref):
    c = cond_ref[...]
    s = (c * jax.nn.sigmoid(c)).astype(BF16)
    o_ref[...] = _dot(s, w_ref[...].astype(BF16)) + b_ref[...]


def _mod_call(cond, w_mod, b_mod):
    return pl.pallas_call(
        _mod_kernel,
        grid=(DEPTH, N_MOD // MOD_TN),
        in_specs=[
            pl.BlockSpec((MOD_ROWS, D_MODEL), lambda l, n: (0, 0)),
            pl.BlockSpec((None, D_MODEL, MOD_TN), lambda l, n: (l, 0, n)),
            pl.BlockSpec((None, 1, MOD_TN), lambda l, n: (l, 0, n)),
        ],
        out_specs=pl.BlockSpec((None, MOD_ROWS, MOD_TN), lambda l, n: (l, 0, n)),
        out_shape=jax.ShapeDtypeStruct((DEPTH, MOD_ROWS, N_MOD), F32),
        compiler_params=_params(("arbitrary", "arbitrary")),
        name="mod",
    )(cond, w_mod, b_mod.reshape(DEPTH, 1, N_MOD))


def _mod_spec(layer, latent, tm, ngrid):
    if latent:
        per_batch = DEC_SEQ // tm
        idx = lambda i: 1 + i // per_batch
    else:
        idx = lambda i: 0
    if ngrid == 1:
        return pl.BlockSpec((None, None, 6, D_MODEL), lambda i: (layer, idx(i), 0, 0))
    return pl.BlockSpec((None, None, 6, D_MODEL), lambda i, j: (layer, idx(i), 0, 0))


IN_TM = 256


def _rope(seg, cos, sin_lo, sin_hi, off):
    return (seg * cos + pltpu.roll(seg, LANES - off, 1) * sin_lo + pltpu.roll(seg, off, 1) * sin_hi)


def _inproj_kernel(*refs, latent, nbt):
    if latent:
        (x_ref, mod_ref, g_ref, w_ref, qkg_ref, ra_ref, rd_ref, ub_ref, uf_ref, h_scr) = refs
        ca_ref = cc_ref = cd_ref = None
    else:
        (x_ref, mod_ref, g_ref, w_ref, qkg_ref, _, _, _, ub_ref, uf_ref, ca_ref, cc_ref, cd_ref, h_scr) = refs

    y = _rms(x_ref[...]) * g_ref[0:1, :]
    h_scr[...] = (y * (1.0 + mod_ref[1:2, :]) + mod_ref[0:1, :]).astype(BF16)

    def rope_a(seg):
        return _rope(seg, ra_ref[0], ra_ref[1], ra_ref[2], DA_DH // 4) if latent else seg

    def rope_d(seg):
        return _rope(seg, rd_ref[0], rd_ref[1], rd_ref[2], HEAD_DIM // 4) if latent else seg

    def to_cache(c_ref, kv, lane0, val):
        if latent:
            return
        for bb in range(nbt):
            c_ref[bb, kv, :, lane0:lane0 + LANES] = val[bb * SEQ:(bb + 1) * SEQ]

    for col in range(N_COL_BLOCKS):
        c0 = col * COL_BLOCK
        acc = _dot(h_scr[...], w_ref[:, c0:c0 + COL_BLOCK])
        if col in (COL_BX, COL_BG):
            uf_ref[:, (col - COL_BX) * COL_BLOCK:(col - COL_BX + 1) * COL_BLOCK] = acc
            ub_ref[:, c0:c0 + COL_BLOCK] = acc.astype(BF16)
            continue
        for hh in range(HEADS_PER_BLOCK):
            l0 = hh * LANES
            seg = acc[:, l0:l0 + LANES]
            if col in (COL_AQ, COL_AK):
                seg = rope_a(seg)
            elif col == COL_DQ:
                seg = rope_d(_rms(seg) * qkg_ref[0:1, :])
            elif col == COL_DKV and hh < GQA_KV_HEADS:
                seg = rope_d(_rms(seg) * qkg_ref[1:2, :])
            ub_ref[:, c0 + l0:c0 + l0 + LANES] = seg.astype(BF16)
            if col in (COL_AK, COL_AV):
                to_cache(ca_ref, col - COL_AK, l0, seg)
            elif col in (COL_CK, COL_CV):
                to_cache(cc_ref, col - COL_CK, l0, seg)
            elif col == COL_DKV:
                to_cache(cd_ref, hh // GQA_KV_HEADS, (hh % GQA_KV_HEADS) * LANES, seg)


def _inproj_call(x, mod, norm_g, w_in_b, qk_g, rope_tabs, caches, layer, *, latent):
    rows = x.shape[0]
    tm = IN_TM
    in_specs = [
        pl.BlockSpec((tm, D_MODEL), lambda i: (i, 0)),
        _mod_spec(layer, latent, tm, 1),
        pl.BlockSpec((None, 4, D_MODEL), lambda i: (layer, 0, 0)),
        pl.BlockSpec((None, D_MODEL, IN_WIDTH), lambda i: (layer, 0, 0), pipeline_mode=RESIDENT),
        pl.BlockSpec((None, 2, HEAD_DIM), lambda i: (layer, 0, 0)),
    ]
    args = [x, mod, norm_g, w_in_b, qk_g]
    out_specs = [
        pl.BlockSpec((tm, IN_WIDTH), lambda i: (i, 0)),
        pl.BlockSpec((tm, 2 * COL_BLOCK), lambda i: (i, 0)),
    ]
    out_shape = [
        jax.ShapeDtypeStruct((rows, IN_WIDTH), BF16),
        jax.ShapeDtypeStruct((rows, 2 * COL_BLOCK), F32),
    ]
    aliases = {}
    nbt = 1
    if latent:
        per_batch = DEC_SEQ // tm
        tab_spec = pl.BlockSpec((3, tm, LANES), lambda i: (0, i % per_batch, 0))
        in_specs += [tab_spec, tab_spec]
        args += [rope_tabs[0], rope_tabs[1]]
    else:
        nbt = tm // SEQ
        widths = (DA_HEADS * HEAD_DIM, NA_HEADS * HEAD_DIM, GQA_KV_HEADS * HEAD_DIM)
        for n, wdt in enumerate(widths):
            out_specs.append(pl.BlockSpec((nbt, None, 2, SEQ, wdt), lambda i: (i, layer, 0, 0, 0)))
            out_shape.append(jax.ShapeDtypeStruct((BATCH, DEPTH, 2, SEQ, wdt), F32))
            in_specs.append(pl.BlockSpec(memory_space=pl.ANY))
            if caches is None:
                args.append(jnp.zeros((8, LANES), F32))
            else:
                args.append(caches[n])
                aliases[len(args) - 1] = 2 + n
    return pl.pallas_call(
        functools.partial(_inproj_kernel, latent=latent, nbt=nbt),
        grid=(rows // tm,),
        in_specs=in_specs,
        out_specs=out_specs,
        out_shape=out_shape,
        input_output_aliases=aliases,
        scratch_shapes=[pltpu.VMEM((tm, D_MODEL), BF16)],
        compiler_params=_params(("arbitrary",)),
        name="inproj_lat" if latent else "inproj_ctx",
    )(*args)


def _rope_tables():
    t = jnp.arange(DEC_SEQ)
    row = (t // GRID_W).astype(F32)[:, None]
    col = (t % GRID_W).astype(F32)[:, None]
    lane = np.arange(LANES)

    def table(d):
        m = d // 2
        freqs = ROPE_THETA ** (-jnp.arange(0, m, 2, dtype=F32) / m)
        within = lane % m
        k = within % (m // 2)
        hi = (within >= m // 2)[None, :]
        use_col = ((lane % d) >= m)[None, :]
        ang = jnp.where(use_col, col, row) * freqs[k][None, :]
        cos, sin = jnp.cos(ang), jnp.sin(ang)
        return jnp.stack([cos, jnp.where(hi, 0.0, -sin), jnp.where(hi, sin, 0.0)])

    return table(DA_DH), table(HEAD_DIM)


def _softmax_pieces(pieces, scale):
    m = pieces[0].max(axis=-1, keepdims=True)
    for s in pieces[1:]:
        m = jnp.maximum(m, s.max(axis=-1, keepdims=True))
    c = scale * LOG2E
    es = [jnp.exp2((s - m) * c) for s in pieces]
    l = es[0].sum(axis=-1, keepdims=True)
    for e in es[1:]:
        l = l + e.sum(axis=-1, keepdims=True)
    return es, 1.0 / l


ATT_TQ = 256


def _diff_kernel(*refs, lam_init, latent):
    if latent:
        lamw_ref, q_ref, k_ref, v_ref, kc_ref, vc_ref, o_ref = refs
    else:
        lamw_ref, q_ref, k_ref, v_ref, o_ref = refs
    lw = lamw_ref[...]
    lam = (jnp.exp(jnp.sum(lw[0:1] * lw[1:2], axis=-1, keepdims=True))
           - jnp.exp(jnp.sum(lw[2:3] * lw[3:4], axis=-1, keepdims=True)) + lam_init)
    q = q_ref[...]
    lane = lax.broadcasted_iota(jnp.int32, q.shape, 1)
    zero = jnp.zeros_like(q)
    q1 = jnp.where(lane < DA_DH, q, zero)
    q2 = jnp.where(lane >= DA_DH, q, zero)
    ks = [k_ref[...]]
    vs = [v_ref[...]]
    if latent:
        ks = [kc_ref[...].astype(BF16)] + ks
        vs = [vc_ref[...].astype(BF16)] + vs
    scale = DA_DH ** -0.5
    e1, r1 = _softmax_pieces([_dot_nt(q1, k) for k in ks], scale)
    e2, r2 = _softmax_pieces([_dot_nt(q2, k) for k in ks], scale)
    r2 = r2 * lam
    o = None
    for a, b, v in zip(e1, e2, vs):
        p = (a * r1 - b * r2).astype(BF16)
        c = _dot(p, v)
        o = c if o is None else o + c
    o_ref[...] = (_rms(o) * (1.0 - lam_init)).astype(BF16)


def _cache_specs(layer, head_of):
    return [pl.BlockSpec((None, None, None, PAST_LEN, LANES),
                         lambda b, h, i, kv=kv: (b, layer, kv, 0, head_of(h))) for kv in (0, 1)]


def _diff_call(ub, lam_w, cache, layer, *, latent):
    s = DEC_SEQ if latent else SEQ
    nb = DEC_BATCH if latent else BATCH
    tq = ATT_TQ
    nq = s // tq
    lam_init = 0.8 - 0.6 * math.exp(-0.3 * layer)
    in_specs = [
        pl.BlockSpec((None, 4, DA_DH), lambda b, h, i: (layer, 0, 0)),
        pl.BlockSpec((tq, LANES), lambda b, h, i: (b * nq + i, COL_AQ * 4 + h)),
        pl.BlockSpec((s, LANES), lambda b, h, i: (b, COL_AK * 4 + h)),
        pl.BlockSpec((s, LANES), lambda b, h, i: (b, COL_AV * 4 + h)),
    ]
    args = [lam_w, ub, ub, ub]
    if latent:
        in_specs += _cache_specs(layer, lambda h: h)
        args += [cache, cache]
    return pl.pallas_call(
        functools.partial(_diff_kernel, lam_init=lam_init, latent=latent),
        grid=(nb, DA_HEADS, nq),
        in_specs=in_specs,
        out_specs=pl.BlockSpec((tq, LANES), lambda b, h, i: (b * nq + i, h)),
        out_shape=jax.ShapeDtypeStruct((nb * s, DA_HEADS * HEAD_DIM), BF16),
        compiler_params=_params(("arbitrary", "arbitrary", "arbitrary")),
        name="diff_lat" if latent else "diff_ctx",
    )(*args)


def _gqa_kernel(*refs, group, latent):
    if latent:
        q_ref, k_ref, v_ref, kc_ref, vc_ref, o_ref = refs
    else:
        q_ref, k_ref, v_ref, o_ref = refs
    ks = [k_ref[...]]
    vs = [v_ref[...]]
    if latent:
        ks = [kc_ref[...].astype(BF16)] + ks
        vs = [vc_ref[...].astype(BF16)] + vs
    scale = HEAD_DIM ** -0.5
    for g in range(group):
        sl = slice(g * LANES, (g + 1) * LANES)
        q = q_ref[:, sl]
        es, r = _softmax_pieces([_dot_nt(q, k) for k in ks], scale)
        o = None
        for e, v in zip(es, vs):
            c = _dot(e.astype(BF16), v)
            o = c if o is None else o + c
        o_ref[:, sl] = (o * r).astype(BF16)


def _gqa_call(ub, cache, layer, *, col_q, col_k, col_v, kv_heads, group, latent, name):
    s = DEC_SEQ if latent else SEQ
    nb = DEC_BATCH if latent else BATCH
    tq = ATT_TQ
    nq = s // tq
    qw = group * LANES
    in_specs = [
        pl.BlockSpec((tq, qw), lambda b, n, i: (b * nq + i, col_q // group + n)),
        pl.BlockSpec((s, LANES), lambda b, n, i: (b, col_k + n)),
        pl.BlockSpec((s, LANES), lambda b, n, i: (b, col_v + n)),
    ]
    args = [ub, ub, ub]
    if latent:
        in_specs += _cache_specs(layer, lambda n: n)
        args += [cache, cache]
    return pl.pallas_call(
        functools.partial(_gqa_kernel, group=group, latent=latent),
        grid=(nb, kv_heads, nq),
        in_specs=in_specs,
        out_specs=pl.BlockSpec((tq, qw), lambda b, n, i: (b * nq + i, n)),
        out_shape=jax.ShapeDtypeStruct((nb * s, kv_heads * qw), BF16),
        compiler_params=_params(("arbitrary", "arbitrary", "arbitrary")),
        name=name,
    )(*args)


def _na_win_row0(r0):
    return np.clip(r0 - NA_ROWS // 2, 0, GRID_H - NA_WIN_ROWS)


def _na_bias_tables(rpb):
    nd = 2 * NA_COLS - 1
    lo = (GRID_W - 1) - (NA_COLS - 1)
    period = 2 * GRID_W
    p = jnp.pad(rpb, ((0, 0), (0, 0), (0, 0), (lo, period - nd - lo)))
    tiled = jnp.tile(p, (1, 1, 1, GRID_W))[..., :GRID_W * (period - 1)]
    toe = tiled.reshape(DEPTH, NA_HEADS, 2 * NA_ROWS - 1, GRID_W, period - 1)[..., GRID_W - 1:]
    qc = np.arange(GRID_W)
    cstart = np.clip(qc - NA_COLS // 2, 0, GRID_W - NA_COLS)
    col_ok = (qc[None, :] >= cstart[:, None]) & (qc[None, :] < cstart[:, None] + NA_COLS)
    toe = jnp.where(col_ok, toe, NEG_BIG)
    masked = jnp.full((DEPTH, NA_HEADS, GRID_W, GRID_W), NEG_BIG, F32)
    cases = []
    for r0 in (0, NA_QROWS, GRID_H - NA_QROWS):
        w0 = int(_na_win_row0(r0))
        rows = []
        for qa in range(NA_QROWS):
            qr = r0 + qa
            rstart = int(np.clip(qr - NA_ROWS // 2, 0, GRID_H - NA_ROWS))
            blocks = []
            for kj in range(NA_WIN_ROWS):
                kr = w0 + kj
                inside = rstart <= kr < rstart + NA_ROWS
                blocks.append(toe[:, :, kr - qr + NA_ROWS - 1] if inside else masked)
            rows.append(jnp.concatenate(blocks, axis=-1))
        cases.append(jnp.concatenate(rows, axis=-2))
    return jnp.stack(cases, axis=2)


def _na_kernel(q_ref, k_ref, v_ref, kc_ref, vc_ref, bias_ref, o_ref):
    rb = pl.program_id(2)
    w0 = pl.multiple_of(jnp.clip(rb * NA_QROWS - NA_ROWS // 2, 0, GRID_H - NA_WIN_ROWS) * GRID_W, GRID_W)
    q = q_ref[...]
    kw = k_ref[pl.ds(w0, NA_WIN), :]
    vw = v_ref[pl.ds(w0, NA_WIN), :]
    scale = HEAD_DIM ** -0.5
    s_loc = _dot_nt(q, kw) * scale + bias_ref[...]
    s_ctx = _dot_nt(q, kc_ref[...].astype(BF16)) * scale
    (e_loc, e_ctx), r = _softmax_pieces([s_loc, s_ctx], 1.0)
    o = _dot(e_loc.astype(BF16), vw) + _dot(e_ctx.astype(BF16), vc_ref[...].astype(BF16))
    o_ref[...] = (o * r).astype(BF16)


def _na_call(ub, cache, bias_tabs, layer):
    nrb = GRID_H // NA_QROWS
    case = lambda rb: jnp.where(rb == 0, 0, jnp.where(rb == nrb - 1, 2, 1))
    return pl.pallas_call(
        _na_kernel,
        grid=(DEC_BATCH, NA_HEADS, nrb),
        in_specs=[
            pl.BlockSpec((NA_TQ, LANES), lambda b, h, r: (b * nrb + r, COL_CQ * 4 + h)),
            pl.BlockSpec((DEC_SEQ, LANES), lambda b, h, r: (b, COL_CK * 4 + h)),
            pl.BlockSpec((DEC_SEQ, LANES), lambda b, h, r: (b, COL_CV * 4 + h)),
            *_cache_specs(layer, lambda h: h),
            pl.BlockSpec((None, None, None, NA_TQ, NA_WIN), lambda b, h, r: (layer, h, case(r), 0, 0)),
        ],
        out_specs=pl.BlockSpec((NA_TQ, LANES), lambda b, h, r: (b * nrb + r, h)),
        out_shape=jax.ShapeDtypeStruct((DEC_BATCH * DEC_SEQ, NA_HEADS * HEAD_DIM), BF16),
        compiler_params=_params(("arbitrary", "arbitrary", "arbitrary")),
        name="na_lat",
    )(ub, ub, ub, cache, cache, bias_tabs)


LRU_TC = 256
SUB = 8
PAD = 8


def _gelu_tanh(x):
    return x * (0.5 * (1.0 + jnp.tanh(math.sqrt(2.0 / math.pi) * (x + 0.044715 * (x * x * x)))))


def _lru_kernel(x_ref, g_ref, cw_ref, cb_ref, gw_ref, gb_ref, lam_ref, h0_ref,
                y_ref, st_ref, xp_scr, xc_scr, hf_scr, a_scr, b_scr, *, seq):
    n_chunks = seq // LRU_TC
    zeros_pad = jnp.zeros((PAD, LRU_WIDTH), F32)
    xp_scr[0:PAD, :] = zeros_pad
    xp_scr[PAD + seq:2 * PAD + seq, :] = zeros_pad
    xp_scr[PAD:PAD + seq, :] = x_ref[...]

    rowid = lax.broadcasted_iota(jnp.int32, (SUB, LRU_WIDTH), 0)

    def coeffs(xc, d):
        xcb = xc.astype(BF16)
        pre = []
        for gi in range(2):
            parts = [_dot(xcb[:, n * LRU_BS:(n + 1) * LRU_BS], gw_ref[d, gi, n].astype(BF16))
                     for n in range(LRU_BLOCKS)]
            pre.append(jnp.concatenate(parts, axis=-1) + gb_ref[d, gi:gi + 1, :])
        r = jax.nn.sigmoid(pre[0])
        i = jax.nn.sigmoid(pre[1])
        log_a = -LRU_C * r * jax.nn.softplus(-lam_ref[d:d + 1, :])
        a = jnp.exp(log_a)
        mult = jnp.sqrt(-jnp.tanh(log_a) * (a * a + 1.0))
        return a, mult * i * xc

    def scan_tile(a, b, carry, reverse):
        for d in (1, 2, 4):
            if reverse:
                keep = rowid < SUB - d
                sh = SUB - d
            else:
                keep = rowid >= d
                sh = d
            a_s = jnp.where(keep, pltpu.roll(a, sh, 0), 1.0)
            b_s = jnp.where(keep, pltpu.roll(b, sh, 0), 0.0)
            b = a * b_s + b
            a = a * a_s
        return a * carry + b

    carry = h0_ref[0:1, :]
    for c in range(n_chunks):
        r0 = c * LRU_TC
        xc = cb_ref[...] + sum(
            xp_scr[PAD + r0 + j - CONV_LEFT:PAD + r0 + j - CONV_LEFT + LRU_TC, :] * cw_ref[j:j + 1, :]
            for j in range(CONV_W))
        xc_scr[r0:r0 + LRU_TC, :] = xc
        a, b = coeffs(xc, 0)
        a_scr[...] = a
        b_scr[...] = b

        def body(t, carry, r0=r0):
            r = pl.multiple_of(t * SUB, SUB)
            h = scan_tile(a_scr[pl.ds(r, SUB), :], b_scr[pl.ds(r, SUB), :], carry, False)
            hf_scr[pl.ds(r0 + r, SUB), :] = h
            return h[SUB - 1:SUB, :]

        carry = lax.fori_loop(0, LRU_TC // SUB, body, carry)
    st_ref[0:1, :] = carry

    carry = h0_ref[1:2, :]
    for c in reversed(range(n_chunks)):
        r0 = c * LRU_TC
        a, b = coeffs(xc_scr[r0:r0 + LRU_TC, :], 1)
        a_scr[...] = a
        b_scr[...] = b

        def body(tt, carry):
            r = pl.multiple_of((LRU_TC // SUB - 1 - tt) * SUB, SUB)
            h = scan_tile(a_scr[pl.ds(r, SUB), :], b_scr[pl.ds(r, SUB), :], carry, True)
            b_scr[pl.ds(r, SUB), :] = h
            return h[0:1, :]

        carry = lax.fori_loop(0, LRU_TC // SUB, body, carry)
        y = (hf_scr[r0:r0 + LRU_TC, :] + b_scr[...]) * _gelu_tanh(g_ref[r0:r0 + LRU_TC, :])
        y_ref[r0:r0 + LRU_TC, :] = y.astype(BF16)
    st_ref[1:2, :] = carry


def _lru_call(uf, conv_w, conv_b, gate_w, gate_b, lam, h0, layer, *, latent):
    seq = DEC_SEQ if latent else SEQ
    nb = DEC_BATCH if latent else BATCH
    if latent:
        h0_spec = pl.BlockSpec((None, None, 2, LRU_WIDTH), lambda b: (b, layer, 0, 0))
    else:
        h0_spec = pl.BlockSpec((None, 2, LRU_WIDTH), lambda b: (b, 0, 0))
    return pl.pallas_call(
        functools.partial(_lru_kernel, seq=seq),
        grid=(nb,),
        in_specs=[
            pl.BlockSpec((seq, LRU_WIDTH), lambda b: (b, 0)),
            pl.BlockSpec((seq, LRU_WIDTH), lambda b: (b, 1)),
            pl.BlockSpec((None, CONV_W, LRU_WIDTH), lambda b: (layer, 0, 0)),
            pl.BlockSpec((None, 1, LRU_WIDTH), lambda b: (layer, 0, 0)),
            pl.BlockSpec((None, 2, 2, LRU_BLOCKS, LRU_BS, LRU_BS), lambda b: (layer, 0, 0, 0, 0, 0)),
            pl.BlockSpec((None, 2, 2, LRU_WIDTH), lambda b: (layer, 0, 0, 0)),
            pl.BlockSpec((None, 2, LRU_WIDTH), lambda b: (layer, 0, 0)),
            h0_spec,
        ],
        out_specs=[
            pl.BlockSpec((seq, LRU_WIDTH), lambda b: (b, 0)),
            pl.BlockSpec((None, 2, LRU_WIDTH), lambda b: (b, 0, 0)),
        ],
        out_shape=[
            jax.ShapeDtypeStruct((nb * seq, LRU_WIDTH), BF16),
            jax.ShapeDtypeStruct((nb, 2, LRU_WIDTH), F32),
        ],
        scratch_shapes=[
            pltpu.VMEM((seq + 2 * PAD, LRU_WIDTH), F32),
            pltpu.VMEM((seq, LRU_WIDTH), F32),
            pltpu.VMEM((seq, LRU_WIDTH), F32),
            pltpu.VMEM((LRU_TC, LRU_WIDTH), F32),
            pltpu.VMEM((LRU_TC, LRU_WIDTH), F32),
        ],
        compiler_params=_params(("arbitrary",)),
        name="lru_lat" if latent else "lru_ctx",
    )(uf, uf, conv_w, conv_b.reshape(DEPTH, 1, LRU_WIDTH), gate_w, gate_b, lam, h0)


OUT_TM = 512


def _outproj_kernel(x_ref, ma_ref, mb_ref, mc_ref, md_ref, mod_ref, g_ref, w_ref, x1_ref, h2_ref):
    y = None
    for n, m_ref in enumerate((ma_ref, mb_ref, mc_ref, md_ref)):
        c = _dot(m_ref[...], w_ref[n * COL_BLOCK:(n + 1) * COL_BLOCK, :])
        y = c if y is None else y + c
    x1 = x_ref[...] + mod_ref[2:3, :] * (_rms(y) * g_ref[1:2, :])
    x1_ref[...] = x1
    h2 = _rms(x1) * g_ref[2:3, :]
    h2_ref[...] = (h2 * (1.0 + mod_ref[4:5, :]) + mod_ref[3:4, :]).astype(BF16)


def _outproj_call(x, parts, mod, norm_g, w_out_b, layer, *, latent):
    rows = x.shape[0]
    tm = OUT_TM
    part_spec = pl.BlockSpec((tm, COL_BLOCK), lambda i: (i, 0))
    return pl.pallas_call(
        _outproj_kernel,
        grid=(rows // tm,),
        in_specs=[
            pl.BlockSpec((tm, D_MODEL), lambda i: (i, 0)),
            part_spec, part_spec, part_spec, part_spec,
            _mod_spec(layer, latent, tm, 1),
            pl.BlockSpec((None, 4, D_MODEL), lambda i: (layer, 0, 0)),
            pl.BlockSpec((None, MIX_WIDTH, D_MODEL), lambda i: (layer, 0, 0), pipeline_mode=RESIDENT),
        ],
        out_specs=[
            pl.BlockSpec((tm, D_MODEL), lambda i: (i, 0)),
            pl.BlockSpec((tm, D_MODEL), lambda i: (i, 0)),
        ],
        out_shape=[
            jax.ShapeDtypeStruct((rows, D_MODEL), F32),
            jax.ShapeDtypeStruct((rows, D_MODEL), BF16),
        ],
        compiler_params=_params(("arbitrary",)),
        name="outproj_lat" if latent else "outproj_ctx",
    )(x, *parts, mod, norm_g, w_out_b)


FFN_TM = 512
FFN_TH = 512


def _ffn_kernel(h_ref, x1_ref, mod_ref, g_ref, wa_ref, wg_ref, wo_ref, o_ref, acc_scr):
    j = pl.program_id(1)
    h = h_ref[...]
    a = _dot(h, wa_ref[...])
    g = _dot(h, wg_ref[...])
    t = ((g * jax.nn.sigmoid(g)) * a).astype(BF16)
    c = _dot(t, wo_ref[...])

    @pl.when(j == 0)
    def _():
        acc_scr[...] = c

    @pl.when(j > 0)
    def _():
        acc_scr[...] += c

    @pl.when(j == pl.num_programs(1) - 1)
    def _():
        o_ref[...] = x1_ref[...] + mod_ref[5:6, :] * (_rms(acc_scr[...]) * g_ref[3:4, :])


def _ffn_call(h2, x1, mod, norm_g, w_ffn_in_b, w_ffn_out_b, layer, *, latent):
    rows = x1.shape[0]
    tm, th = FFN_TM, FFN_TH
    nh = FFN_HIDDEN // th
    return pl.pallas_call(
        _ffn_kernel,
        grid=(rows // tm, nh),
        in_specs=[
            pl.BlockSpec((tm, D_MODEL), lambda i, j: (i, 0)),
            pl.BlockSpec((tm, D_MODEL), lambda i, j: (i, 0)),
            _mod_spec(layer, latent, tm, 2),
            pl.BlockSpec((None, 4, D_MODEL), lambda i, j: (layer, 0, 0)),
            pl.BlockSpec((None, D_MODEL, th), lambda i, j: (layer, 0, j)),
            pl.BlockSpec((None, D_MODEL, th), lambda i, j: (layer, 0, nh + j)),
            pl.BlockSpec((None, th, D_MODEL), lambda i, j: (layer, j, 0)),
        ],
        out_specs=pl.BlockSpec((tm, D_MODEL), lambda i, j: (i, 0)),
        out_shape=jax.ShapeDtypeStruct((rows, D_MODEL), F32),
        scratch_shapes=[pltpu.VMEM((tm, D_MODEL), F32)],
        compiler_params=_params(("arbitrary", "arbitrary")),
        name="ffn_lat" if latent else "ffn_ctx",
    )(h2, x1, mod, norm_g, w_ffn_in_b, w_ffn_in_b, w_ffn_out_b)


def kernel(x_prompt, x_sample, cache_diff_kv, state_rglru, cache_na_kv, cache_gqa_kv, c, c_ctx, w_mod, b_mod,
           norm_g, w_in, diff_lambda_w, lru_conv_w, lru_conv_b, lru_gate_w, lru_gate_b, lru_lambda, na_rpb,
           gqa_qk_g, w_out, w_ffn_in, w_ffn_out):
    cond = jnp.concatenate([c_ctx[None, :], c, jnp.zeros((MOD_ROWS - 1 - DEC_BATCH, D_MODEL), F32)], axis=0)
    mod = _mod_call(cond, w_mod, b_mod).reshape(DEPTH, MOD_ROWS, 6, D_MODEL)

    w_in_b = w_in.astype(BF16)
    w_out_b = w_out.astype(BF16)
    w_ffn_in_b = w_ffn_in.astype(BF16)
    w_ffn_out_b = w_ffn_out.astype(BF16)
    rope_tabs = _rope_tables()
    na_bias = _na_bias_tables(na_rpb)

    cache_a = cache_diff_kv.reshape(DEC_BATCH, DEPTH, 2, PAST_LEN, DA_HEADS * HEAD_DIM)
    cache_c = cache_na_kv.reshape(DEC_BATCH, DEPTH, 2, PAST_LEN, NA_HEADS * HEAD_DIM)
    cache_d = cache_gqa_kv.reshape(DEC_BATCH, DEPTH, 2, PAST_LEN, GQA_KV_HEADS * HEAD_DIM)
    zero_state = jnp.zeros((BATCH, 2, LRU_WIDTH), F32)

    def layer(x, l, latent, new_caches):
        outs = _inproj_call(x, mod, norm_g, w_in_b, gqa_qk_g, rope_tabs, new_caches, l, latent=latent)
        ub, uf = outs[0], outs[1]
        m_a = _diff_call(ub, diff_lambda_w, cache_a, l, latent=latent)
        m_b, st = _lru_call(uf, lru_conv_w, lru_conv_b, lru_gate_w, lru_gate_b, lru_lambda,
                            state_rglru if latent else zero_state, l, latent=latent)
        if latent:
            m_c = _na_call(ub, cache_c, na_bias, l)
        else:
            m_c = _gqa_call(ub, None, l, col_q=COL_CQ * 4, col_k=COL_CK * 4, col_v=COL_CV * 4,
                            kv_heads=NA_HEADS, group=1, latent=False, name="attn_c_ctx")
        m_d = _gqa_call(ub, cache_d, l, col_q=COL_DQ * 4, col_k=COL_DKV * 4, col_v=COL_DKV * 4 + GQA_KV_HEADS,
                        kv_heads=GQA_KV_HEADS, group=GQA_Q_HEADS // GQA_KV_HEADS, latent=latent,
                        name="attn_d_lat" if latent else "attn_d_ctx")
        x1, h2 = _outproj_call(x, (m_a, m_b, m_c, m_d), mod, norm_g, w_out_b, l, latent=latent)
        x_new = _ffn_call(h2, x1, mod, norm_g, w_ffn_in_b, w_ffn_out_b, l, latent=latent)
        return x_new, tuple(outs[2:]), st

    y = x_prompt.reshape(BATCH * SEQ, D_MODEL)
    new_caches = None
    states = []
    for l in range(DEPTH):
        y, new_caches, st = layer(y, l, False, new_caches)
        states.append(st)
    y_prompt = y.reshape(BATCH, SEQ, D_MODEL)
    new_cache_diff_kv = new_caches[0].reshape(BATCH, DEPTH, 2, SEQ, DA_HEADS, HEAD_DIM)
    new_state_rglru = jnp.stack(states, axis=1)
    new_cache_na_kv = new_caches[1].reshape(BATCH, DEPTH, 2, SEQ, NA_HEADS, HEAD_DIM)
    new_cache_gqa_kv = new_caches[2].reshape(BATCH, DEPTH, 2, SEQ, GQA_KV_HEADS, HEAD_DIM)

    z = x_sample.reshape(DEC_BATCH * DEC_SEQ, D_MODEL)
    for l in range(DEPTH):
        z, _, _ = layer(z, l, True, None)
    y_sample = z.reshape(DEC_BATCH, DEC_SEQ, D_MODEL)

    return (y_prompt, y_sample, new_cache_diff_kv, new_state_rglru, new_cache_na_kv, new_cache_gqa_kv)
```

```python
import functools
import math

import jax
import jax.numpy as jnp
import numpy as np
from jax import lax
from jax.experimental import pallas as pl
from jax.experimental.pallas import tpu as pltpu

D_MODEL = 2048
BATCH = 16
SEQ = 256
DEPTH = 4
DEC_BATCH = 4
DEC_SEQ = 2048
PAST_LEN = 512
GRID_W = 64
GRID_H = DEC_SEQ // GRID_W
HEAD_DIM = 128
EPS = 1e-6
ROPE_THETA = 10000.0
DA_HEADS = 4
DA_DH = HEAD_DIM // 2
LRU_WIDTH = 512
LRU_BLOCKS = 4
LRU_BS = LRU_WIDTH // LRU_BLOCKS
LRU_C = 8.0
CONV_W = 4
CONV_LEFT = 2
NA_HEADS = 4
NA_ROWS = 8
NA_COLS = 16
GQA_Q_HEADS = 4
GQA_KV_HEADS = 2
IN_WIDTH = 5120
MIX_WIDTH = 2048
FFN_HIDDEN = 5632
N_MOD = 6 * D_MODEL

F32 = jnp.float32
BF16 = jnp.bfloat16

VMEM_LIMIT_BYTES = 56 * 1024 * 1024
LANES = 128
LOG2E = math.log2(math.e)

COL_AQ, COL_AK, COL_AV, COL_BX, COL_BG, COL_CQ, COL_CK, COL_CV, COL_DQ, COL_DKV = range(10)
N_COL_BLOCKS = 10
COL_BLOCK = 512
HEADS_PER_BLOCK = COL_BLOCK // LANES

NA_QROWS = 4
NA_WIN_ROWS = NA_QROWS + NA_ROWS
NA_TQ = NA_QROWS * GRID_W
NA_WIN = NA_WIN_ROWS * GRID_W
NEG_BIG = -1e30

RESIDENT = pl.Buffered(1)


def _params(sem):
    return pltpu.CompilerParams(dimension_semantics=sem, vmem_limit_bytes=VMEM_LIMIT_BYTES)


def _rms(x):
    return x * lax.rsqrt(jnp.mean(x * x, axis=-1, keepdims=True) + EPS)


def _dot(a, b):
    return jnp.dot(a, b, preferred_element_type=F32)


def _dot_nt(a, b):
    return lax.dot_general(a, b, (((1,), (1,)), ((), ())), preferred_element_type=F32)


MOD_ROWS = 8
MOD_TN = 1024


def _mod_kernel(cond_ref, w_ref, b_ref, o_ref):
    c = cond_ref[...]
    s = (c * jax.nn.sigmoid(c)).astype(BF16)
    o_ref[...] = _dot(s, w_ref[...].astype(BF16)) + b_ref[...]


def _mod_call(cond, w_mod, b_mod):
    return pl.pallas_call(
        _mod_kernel,
        grid=(DEPTH, N_MOD // MOD_TN),
        in_specs=[
            pl.BlockSpec((MOD_ROWS, D_MODEL), lambda l, n: (0, 0)),
            pl.BlockSpec((None, D_MODEL, MOD_TN), lambda l, n: (l, 0, n)),
            pl.BlockSpec((None, 1, MOD_TN), lambda l, n: (l, 0, n)),
        ],
        out_specs=pl.BlockSpec((None, MOD_ROWS, MOD_TN), lambda l, n: (l, 0, n)),
        out_shape=jax.ShapeDtypeStruct((DEPTH, MOD_ROWS, N_MOD), F32),
        compiler_params=_params(("arbitrary", "arbitrary")),
        name="mod",
    )(cond, w_mod, b_mod.reshape(DEPTH, 1, N_MOD))


def _mod_spec(layer, latent, tm, ngrid):
    if latent:
        per_batch = DEC_SEQ // tm
        idx = lambda i: 1 + i // per_batch
    else:
        idx = lambda i: 0
    if ngrid == 1:
        return pl.BlockSpec((None, None, 6, D_MODEL), lambda i: (layer, idx(i), 0, 0))
    return pl.BlockSpec((None, None, 6, D_MODEL), lambda i, j: (layer, idx(i), 0, 0))


IN_TM = 256


def _rope(seg, cos, sin_lo, sin_hi, off):
    return (seg * cos + pltpu.roll(seg, LANES - off, 1) * sin_lo + pltpu.roll(seg, off, 1) * sin_hi)


def _inproj_kernel(*refs, latent, nbt):
    if latent:
        (x_ref, mod_ref, g_ref, w_ref, qkg_ref, ra_ref, rd_ref, ub_ref, uf_ref, h_scr) = refs
        ca_ref = cc_ref = cd_ref = None
    else:
        (x_ref, mod_ref, g_ref, w_ref, qkg_ref, _, _, _, ub_ref, uf_ref, ca_ref, cc_ref, cd_ref, h_scr) = refs

    y = _rms(x_ref[...]) * g_ref[0:1, :]
    h_scr[...] = (y * (1.0 + mod_ref[1:2, :]) + mod_ref[0:1, :]).astype(BF16)

    def rope_a(seg):
        return _rope(seg, ra_ref[0], ra_ref[1], ra_ref[2], DA_DH // 4) if latent else seg

    def rope_d(seg):
        return _rope(seg, rd_ref[0], rd_ref[1], rd_ref[2], HEAD_DIM // 4) if latent else seg

    def to_cache(c_ref, kv, lane0, val):
        if latent:
            return
        for bb in range(nbt):
            c_ref[bb, kv, :, lane0:lane0 + LANES] = val[bb * SEQ:(bb + 1) * SEQ]

    for col in range(N_COL_BLOCKS):
        c0 = col * COL_BLOCK
        acc = _dot(h_scr[...], w_ref[:, c0:c0 + COL_BLOCK])
        if col in (COL_BX, COL_BG):
            uf_ref[:, (col - COL_BX) * COL_BLOCK:(col - COL_BX + 1) * COL_BLOCK] = acc
            ub_ref[:, c0:c0 + COL_BLOCK] = acc.astype(BF16)
            continue
        for hh in range(HEADS_PER_BLOCK):
            l0 = hh * LANES
            seg = acc[:, l0:l0 + LANES]
            if col == COL_AQ:
                seg = rope_a(seg) * (DA_DH ** -0.5 * LOG2E)
            elif col == COL_AK:
                seg = rope_a(seg)
            elif col == COL_CQ:
                seg = seg * (HEAD_DIM ** -0.5 * LOG2E)
            elif col == COL_DQ:
                seg = rope_d(_rms(seg) * qkg_ref[0:1, :]) * (HEAD_DIM ** -0.5 * LOG2E)
            elif col == COL_DKV and hh < GQA_KV_HEADS:
                seg = rope_d(_rms(seg) * qkg_ref[1:2, :])
            ub_ref[:, c0 + l0:c0 + l0 + LANES] = seg.astype(BF16)
            if col in (COL_AK, COL_AV):
                to_cache(ca_ref, col - COL_AK, l0, seg)
            elif col in (COL_CK, COL_CV):
                to_cache(cc_ref, col - COL_CK, l0, seg)
            elif col == COL_DKV:
                to_cache(cd_ref, hh // GQA_KV_HEADS, (hh % GQA_KV_HEADS) * LANES, seg)


def _inproj_call(x, mod, norm_g, w_in_b, qk_g, rope_tabs, caches, layer, *, latent):
    rows = x.shape[0]
    tm = IN_TM
    in_specs = [
        pl.BlockSpec((tm, D_MODEL), lambda i: (i, 0)),
        _mod_spec(layer, latent, tm, 1),
        pl.BlockSpec((None, 4, D_MODEL), lambda i: (layer, 0, 0)),
        pl.BlockSpec((None, D_MODEL, IN_WIDTH), lambda i: (layer, 0, 0), pipeline_mode=RESIDENT),
        pl.BlockSpec((None, 2, HEAD_DIM), lambda i: (layer, 0, 0)),
    ]
    args = [x, mod, norm_g, w_in_b, qk_g]
    out_specs = [
        pl.BlockSpec((tm, IN_WIDTH), lambda i: (i, 0)),
        pl.BlockSpec((tm, 2 * COL_BLOCK), lambda i: (i, 0)),
    ]
    out_shape = [
        jax.ShapeDtypeStruct((rows, IN_WIDTH), BF16),
        jax.ShapeDtypeStruct((rows, 2 * COL_BLOCK), F32),
    ]
    aliases = {}
    nbt = 1
    if latent:
        per_batch = DEC_SEQ // tm
        tab_spec = pl.BlockSpec((3, tm, LANES), lambda i: (0, i % per_batch, 0))
        in_specs += [tab_spec, tab_spec]
        args += [rope_tabs[0], rope_tabs[1]]
    else:
        nbt = tm // SEQ
        widths = (DA_HEADS * HEAD_DIM, NA_HEADS * HEAD_DIM, GQA_KV_HEADS * HEAD_DIM)
        for n, wdt in enumerate(widths):
            out_specs.append(pl.BlockSpec((nbt, None, 2, SEQ, wdt), lambda i: (i, layer, 0, 0, 0)))
            out_shape.append(jax.ShapeDtypeStruct((BATCH, DEPTH, 2, SEQ, wdt), F32))
            in_specs.append(pl.BlockSpec(memory_space=pl.ANY))
            if caches is None:
                args.append(jnp.zeros((8, LANES), F32))
            else:
                args.append(caches[n])
                aliases[len(args) - 1] = 2 + n
    return pl.pallas_call(
        functools.partial(_inproj_kernel, latent=latent, nbt=nbt),
        grid=(rows // tm,),
        in_specs=in_specs,
        out_specs=out_specs,
        out_shape=out_shape,
        input_output_aliases=aliases,
        scratch_shapes=[pltpu.VMEM((tm, D_MODEL), BF16)],
        compiler_params=_params(("arbitrary",)),
        name="inproj_lat" if latent else "inproj_ctx",
    )(*args)


def _rope_tables():
    t = jnp.arange(DEC_SEQ)
    row = (t // GRID_W).astype(F32)[:, None]
    col = (t % GRID_W).astype(F32)[:, None]
    lane = np.arange(LANES)

    def table(d):
        m = d // 2
        freqs = ROPE_THETA ** (-jnp.arange(0, m, 2, dtype=F32) / m)
        within = lane % m
        k = within % (m // 2)
        hi = (within >= m // 2)[None, :]
        use_col = ((lane % d) >= m)[None, :]
        ang = jnp.where(use_col, col, row) * freqs[k][None, :]
        cos, sin = jnp.cos(ang), jnp.sin(ang)
        return jnp.stack([cos, jnp.where(hi, 0.0, -sin), jnp.where(hi, sin, 0.0)])

    return table(DA_DH), table(HEAD_DIM)


def _with_ones(v):
    return jnp.concatenate([v, jnp.ones_like(v)], axis=-1)


def _softmax_av(q, kt, v_ones):
    s = _dot(q, kt)
    e = jnp.exp2(s - s.max(axis=-1, keepdims=True)).astype(BF16)
    acc = _dot(e, v_ones)
    return acc[:, :LANES] * (1.0 / acc[:, LANES:])


def _fill_kv(kt_scr, v_scr, k_ref, v_ref, kc_ref, vc_ref):
    n_new = k_ref.shape[0]
    p = kc_ref.shape[0]
    kt_scr[:, 0:p] = kc_ref[...].T.astype(BF16)
    kt_scr[:, p:p + n_new] = k_ref[...].astype(F32).T.astype(BF16)
    v_scr[0:p, 0:LANES] = vc_ref[...].astype(BF16)
    v_scr[p:p + n_new, 0:LANES] = v_ref[...]
    v_scr[:, LANES:] = jnp.ones((p + n_new, LANES), BF16)


ATT_TQ = 1024
ATT_SUB = 256
LAT_KEYS = PAST_LEN + DEC_SEQ


def _diff_lambda(lamw_ref, lam_init):
    lw = lamw_ref[...]
    return (jnp.exp(jnp.sum(lw[0:1] * lw[1:2], axis=-1, keepdims=True))
            - jnp.exp(jnp.sum(lw[2:3] * lw[3:4], axis=-1, keepdims=True)) + lam_init)


def _diff_head(q, kt, v_ones, lam, lam_init):
    r = q.shape[0]
    lane = lax.broadcasted_iota(jnp.int32, q.shape, 1)
    zero = jnp.zeros_like(q)
    qs = jnp.concatenate([jnp.where(lane < DA_DH, q, zero), jnp.where(lane >= DA_DH, q, zero)], axis=0)
    o = _softmax_av(qs, kt, v_ones)
    return _rms(o[:r] - lam * o[r:]) * (1.0 - lam_init)


def _diff_kernel(lamw_ref, q_ref, k_ref, v_ref, kc_ref, vc_ref, o_ref, kt_scr, v_scr, *, lam_init):
    @pl.when(pl.program_id(2) == 0)
    def _():
        _fill_kv(kt_scr, v_scr, k_ref, v_ref, kc_ref, vc_ref)

    lam = _diff_lambda(lamw_ref, lam_init)
    for c in range(ATT_TQ // ATT_SUB):
        rows = slice(c * ATT_SUB, (c + 1) * ATT_SUB)
        o_ref[rows, :] = _diff_head(q_ref[rows, :], kt_scr[...], v_scr[...], lam, lam_init).astype(BF16)


def _cache_specs(layer, head_of):
    return [pl.BlockSpec((None, None, None, PAST_LEN, LANES),
                         lambda b, h, i, kv=kv: (b, layer, kv, 0, head_of(h))) for kv in (0, 1)]


def _kv_scratch():
    return [pltpu.VMEM((HEAD_DIM, LAT_KEYS), BF16), pltpu.VMEM((LAT_KEYS, 2 * LANES), BF16)]


def _diff_call(ub, lam_w, cache, layer):
    tq = ATT_TQ
    nq = DEC_SEQ // tq
    lam_init = 0.8 - 0.6 * math.exp(-0.3 * layer)
    return pl.pallas_call(
        functools.partial(_diff_kernel, lam_init=lam_init),
        grid=(DEC_BATCH, DA_HEADS, nq),
        in_specs=[
            pl.BlockSpec((None, 4, DA_DH), lambda b, h, i: (layer, 0, 0)),
            pl.BlockSpec((tq, LANES), lambda b, h, i: (b * nq + i, COL_AQ * 4 + h)),
            pl.BlockSpec((DEC_SEQ, LANES), lambda b, h, i: (b, COL_AK * 4 + h)),
            pl.BlockSpec((DEC_SEQ, LANES), lambda b, h, i: (b, COL_AV * 4 + h)),
            *_cache_specs(layer, lambda h: h),
        ],
        out_specs=pl.BlockSpec((tq, LANES), lambda b, h, i: (b * nq + i, h)),
        out_shape=jax.ShapeDtypeStruct((DEC_BATCH * DEC_SEQ, DA_HEADS * HEAD_DIM), BF16),
        scratch_shapes=_kv_scratch(),
        compiler_params=_params(("arbitrary", "arbitrary", "arbitrary")),
        name="diff_lat",
    )(lam_w, ub, ub, ub, cache, cache)


GQA_GROUP = GQA_Q_HEADS // GQA_KV_HEADS


def _gqa_kernel(q_ref, k_ref, v_ref, kc_ref, vc_ref, o_ref, kt_scr, v_scr):
    @pl.when(pl.program_id(2) == 0)
    def _():
        _fill_kv(kt_scr, v_scr, k_ref, v_ref, kc_ref, vc_ref)

    for c in range(ATT_TQ // ATT_SUB):
        r0 = c * ATT_SUB
        qs = jnp.concatenate([q_ref[r0:r0 + ATT_SUB, g * LANES:(g + 1) * LANES] for g in range(GQA_GROUP)], axis=0)
        o = _softmax_av(qs, kt_scr[...], v_scr[...])
        for g in range(GQA_GROUP):
            o_ref[r0:r0 + ATT_SUB, g * LANES:(g + 1) * LANES] = o[g * ATT_SUB:(g + 1) * ATT_SUB].astype(BF16)


def _gqa_call(ub, cache, layer):
    tq = ATT_TQ
    nq = DEC_SEQ // tq
    qw = GQA_GROUP * LANES
    return pl.pallas_call(
        _gqa_kernel,
        grid=(DEC_BATCH, GQA_KV_HEADS, nq),
        in_specs=[
            pl.BlockSpec((tq, qw), lambda b, n, i: (b * nq + i, COL_DQ * 4 // GQA_GROUP + n)),
            pl.BlockSpec((DEC_SEQ, LANES), lambda b, n, i: (b, COL_DKV * 4 + n)),
            pl.BlockSpec((DEC_SEQ, LANES), lambda b, n, i: (b, COL_DKV * 4 + GQA_KV_HEADS + n)),
            *_cache_specs(layer, lambda n: n),
        ],
        out_specs=pl.BlockSpec((tq, qw), lambda b, n, i: (b * nq + i, n)),
        out_shape=jax.ShapeDtypeStruct((DEC_BATCH * DEC_SEQ, GQA_Q_HEADS * HEAD_DIM), BF16),
        scratch_shapes=_kv_scratch(),
        compiler_params=_params(("arbitrary", "arbitrary", "arbitrary")),
        name="attn_d_lat",
    )(ub, ub, ub, cache, cache)


def _attn_ctx_kernel(lamw_ref, u_ref, oa_ref, oc_ref, od_ref, *, lam_init):
    def head(col, hh):
        c0 = col * COL_BLOCK + hh * LANES
        return u_ref[:, c0:c0 + LANES]

    def keys_t(col, hh):
        return head(col, hh).astype(F32).T.astype(BF16)

    lam = _diff_lambda(lamw_ref, lam_init)
    for hh in range(DA_HEADS):
        o = _diff_head(head(COL_AQ, hh), keys_t(COL_AK, hh), _with_ones(head(COL_AV, hh)), lam, lam_init)
        oa_ref[:, hh * LANES:(hh + 1) * LANES] = o.astype(BF16)
    for hh in range(NA_HEADS):
        o = _softmax_av(head(COL_CQ, hh), keys_t(COL_CK, hh), _with_ones(head(COL_CV, hh)))
        oc_ref[:, hh * LANES:(hh + 1) * LANES] = o.astype(BF16)
    for n in range(GQA_KV_HEADS):
        qs = jnp.concatenate([head(COL_DQ, n * GQA_GROUP + g) for g in range(GQA_GROUP)], axis=0)
        o = _softmax_av(qs, keys_t(COL_DKV, n), _with_ones(head(COL_DKV, GQA_KV_HEADS + n)))
        for g in range(GQA_GROUP):
            hq = n * GQA_GROUP + g
            od_ref[:, hq * LANES:(hq + 1) * LANES] = o[g * SEQ:(g + 1) * SEQ].astype(BF16)


def _attn_ctx_call(ub, lam_w, layer):
    lam_init = 0.8 - 0.6 * math.exp(-0.3 * layer)
    out_spec = pl.BlockSpec((SEQ, COL_BLOCK), lambda b: (b, 0))
    out_sds = jax.ShapeDtypeStruct((BATCH * SEQ, COL_BLOCK), BF16)
    return pl.pallas_call(
        functools.partial(_attn_ctx_kernel, lam_init=lam_init),
        grid=(BATCH,),
        in_specs=[
            pl.BlockSpec((None, 4, DA_DH), lambda b: (layer, 0, 0)),
            pl.BlockSpec((SEQ, IN_WIDTH), lambda b: (b, 0)),
        ],
        out_specs=[out_spec, out_spec, out_spec],
        out_shape=[out_sds, out_sds, out_sds],
        compiler_params=_params(("arbitrary",)),
        name="attn_ctx",
    )(lam_w, ub)


def _na_win_row0(r0):
    return np.clip(r0 - NA_ROWS // 2, 0, GRID_H - NA_WIN_ROWS)


def _na_bias_tables(rpb):
    nd = 2 * NA_COLS - 1
    lo = (GRID_W - 1) - (NA_COLS - 1)
    period = 2 * GRID_W
    p = jnp.pad(rpb, ((0, 0), (0, 0), (0, 0), (lo, period - nd - lo)))
    tiled = jnp.tile(p, (1, 1, 1, GRID_W))[..., :GRID_W * (period - 1)]
    toe = tiled.reshape(DEPTH, NA_HEADS, 2 * NA_ROWS - 1, GRID_W, period - 1)[..., GRID_W - 1:]
    qc = np.arange(GRID_W)
    cstart = np.clip(qc - NA_COLS // 2, 0, GRID_W - NA_COLS)
    col_ok = (qc[None, :] >= cstart[:, None]) & (qc[None, :] < cstart[:, None] + NA_COLS)
    toe = jnp.where(col_ok, toe, NEG_BIG)
    masked = jnp.full((DEPTH, NA_HEADS, GRID_W, GRID_W), NEG_BIG, F32)
    cases = []
    for r0 in (0, NA_QROWS, GRID_H - NA_QROWS):
        w0 = int(_na_win_row0(r0))
        rows = []
        for qa in range(NA_QROWS):
            qr = r0 + qa
            rstart = int(np.clip(qr - NA_ROWS // 2, 0, GRID_H - NA_ROWS))
            blocks = []
            for kj in range(NA_WIN_ROWS):
                kr = w0 + kj
                inside = rstart <= kr < rstart + NA_ROWS
                blocks.append(toe[:, :, kr - qr + NA_ROWS - 1] if inside else masked)
            rows.append(jnp.concatenate(blocks, axis=-1))
        cases.append(jnp.concatenate(rows, axis=-2))
    return jnp.stack(cases, axis=2)


NA_CHAINS = 4
NA_BLOCKS = GRID_H // NA_QROWS


def _na_kernel(q_ref, k_ref, v_ref, kc_ref, vc_ref, bias_ref, o_ref, kt_scr, v_scr):
    step = pl.program_id(2)

    @pl.when(step == 0)
    def _():
        _fill_kv(kt_scr, v_scr, k_ref, v_ref, kc_ref, vc_ref)

    for c in range(NA_CHAINS):
        rb = step * NA_CHAINS + c
        rows = slice(c * NA_TQ, (c + 1) * NA_TQ)
        w0 = pl.multiple_of(
            PAST_LEN + jnp.clip(rb * NA_QROWS - NA_ROWS // 2, 0, GRID_H - NA_WIN_ROWS) * GRID_W,
            NA_QROWS * GRID_W)
        case = jnp.where(rb == 0, 0, jnp.where(rb == NA_BLOCKS - 1, 2, 1))
        q = q_ref[rows, :]
        s_loc = _dot(q, kt_scr[:, pl.ds(w0, NA_WIN)]) + bias_ref[case] * LOG2E
        s_ctx = _dot(q, kt_scr[:, 0:PAST_LEN])
        m = jnp.maximum(s_loc.max(axis=-1, keepdims=True), s_ctx.max(axis=-1, keepdims=True))
        acc = (_dot(jnp.exp2(s_loc - m).astype(BF16), v_scr[pl.ds(w0, NA_WIN), :])
               + _dot(jnp.exp2(s_ctx - m).astype(BF16), v_scr[0:PAST_LEN, :]))
        o_ref[rows, :] = (acc[:, :LANES] * (1.0 / acc[:, LANES:])).astype(BF16)


def _na_call(ub, cache, bias_tabs, layer):
    nsteps = NA_BLOCKS // NA_CHAINS
    tq = NA_CHAINS * NA_TQ
    return pl.pallas_call(
        _na_kernel,
        grid=(DEC_BATCH, NA_HEADS, nsteps),
        in_specs=[
            pl.BlockSpec((tq, LANES), lambda b, h, r: (b * nsteps + r, COL_CQ * 4 + h)),
            pl.BlockSpec((DEC_SEQ, LANES), lambda b, h, r: (b, COL_CK * 4 + h)),
            pl.BlockSpec((DEC_SEQ, LANES), lambda b, h, r: (b, COL_CV * 4 + h)),
            *_cache_specs(layer, lambda h: h),
            pl.BlockSpec((None, None, 3, NA_TQ, NA_WIN), lambda b, h, r: (layer, h, 0, 0, 0)),
        ],
        out_specs=pl.BlockSpec((tq, LANES), lambda b, h, r: (b * nsteps + r, h)),
        out_shape=jax.ShapeDtypeStruct((DEC_BATCH * DEC_SEQ, NA_HEADS * HEAD_DIM), BF16),
        scratch_shapes=_kv_scratch(),
        compiler_params=_params(("arbitrary", "arbitrary", "arbitrary")),
        name="na_lat",
    )(ub, ub, ub, cache, cache, bias_tabs)


LRU_TC = 256
SUB = 8
PAD = 8


def _gelu_tanh(x):
    return x * (0.5 * (1.0 + jnp.tanh(math.sqrt(2.0 / math.pi) * (x + 0.044715 * (x * x * x)))))


def _lru_kernel(x_ref, g_ref, cw_ref, cb_ref, gw_ref, gb_ref, lam_ref, h0_ref,
                y_ref, st_ref, xp_scr, xc_scr, hf_scr, a_scr, b_scr, *, seq):
    n_chunks = seq // LRU_TC
    zeros_pad = jnp.zeros((PAD, LRU_WIDTH), F32)
    xp_scr[0:PAD, :] = zeros_pad
    xp_scr[PAD + seq:2 * PAD + seq, :] = zeros_pad
    xp_scr[PAD:PAD + seq, :] = x_ref[...]

    rowid = lax.broadcasted_iota(jnp.int32, (SUB, LRU_WIDTH), 0)

    def coeffs(xc, d):
        xcb = xc.astype(BF16)
        pre = []
        for gi in range(2):
            parts = [_dot(xcb[:, n * LRU_BS:(n + 1) * LRU_BS], gw_ref[d, gi, n].astype(BF16))
                     for n in range(LRU_BLOCKS)]
            pre.append(jnp.concatenate(parts, axis=-1) + gb_ref[d, gi:gi + 1, :])
        r = jax.nn.sigmoid(pre[0])
        i = jax.nn.sigmoid(pre[1])
        log_a = -LRU_C * r * jax.nn.softplus(-lam_ref[d:d + 1, :])
        a = jnp.exp(log_a)
        mult = jnp.sqrt(-jnp.tanh(log_a) * (a * a + 1.0))
        return a, mult * i * xc

    def scan_tile(a, b, carry, reverse):
        for d in (1, 2, 4):
            if reverse:
                keep = rowid < SUB - d
                sh = SUB - d
            else:
                keep = rowid >= d
                sh = d
            a_s = jnp.where(keep, pltpu.roll(a, sh, 0), 1.0)
            b_s = jnp.where(keep, pltpu.roll(b, sh, 0), 0.0)
            b = a * b_s + b
            a = a * a_s
        return a * carry + b

    carry = h0_ref[0:1, :]
    for c in range(n_chunks):
        r0 = c * LRU_TC
        xc = cb_ref[...] + sum(
            xp_scr[PAD + r0 + j - CONV_LEFT:PAD + r0 + j - CONV_LEFT + LRU_TC, :] * cw_ref[j:j + 1, :]
            for j in range(CONV_W))
        xc_scr[r0:r0 + LRU_TC, :] = xc
        a, b = coeffs(xc, 0)
        a_scr[...] = a
        b_scr[...] = b

        def body(t, carry, r0=r0):
            r = pl.multiple_of(t * SUB, SUB)
            h = scan_tile(a_scr[pl.ds(r, SUB), :], b_scr[pl.ds(r, SUB), :], carry, False)
            hf_scr[pl.ds(r0 + r, SUB), :] = h
            return h[SUB - 1:SUB, :]

        carry = lax.fori_loop(0, LRU_TC // SUB, body, carry)
    st_ref[0:1, :] = carry

    carry = h0_ref[1:2, :]
    for c in reversed(range(n_chunks)):
        r0 = c * LRU_TC
        a, b = coeffs(xc_scr[r0:r0 + LRU_TC, :], 1)
        a_scr[...] = a
        b_scr[...] = b

        def body(tt, carry):
            r = pl.multiple_of((LRU_TC // SUB - 1 - tt) * SUB, SUB)
            h = scan_tile(a_scr[pl.ds(r, SUB), :], b_scr[pl.ds(r, SUB), :], carry, True)
            b_scr[pl.ds(r, SUB), :] = h
            return h[0:1, :]

        carry = lax.fori_loop(0, LRU_TC // SUB, body, carry)
        y = (hf_scr[r0:r0 + LRU_TC, :] + b_scr[...]) * _gelu_tanh(g_ref[r0:r0 + LRU_TC, :])
        y_ref[r0:r0 + LRU_TC, :] = y.astype(BF16)
    st_ref[1:2, :] = carry


def _lru_call(uf, conv_w, conv_b, gate_w, gate_b, lam, h0, layer, *, latent):
    seq = DEC_SEQ if latent else SEQ
    nb = DEC_BATCH if latent else BATCH
    if latent:
        h0_spec = pl.BlockSpec((None, None, 2, LRU_WIDTH), lambda b: (b, layer, 0, 0))
    else:
        h0_spec = pl.BlockSpec((None, 2, LRU_WIDTH), lambda b: (b, 0, 0))
    return pl.pallas_call(
        functools.partial(_lru_kernel, seq=seq),
        grid=(nb,),
        in_specs=[
            pl.BlockSpec((seq, LRU_WIDTH), lambda b: (b, 0)),
            pl.BlockSpec((seq, LRU_WIDTH), lambda b: (b, 1)),
            pl.BlockSpec((None, CONV_W, LRU_WIDTH), lambda b: (layer, 0, 0)),
            pl.BlockSpec((None, 1, LRU_WIDTH), lambda b: (layer, 0, 0)),
            pl.BlockSpec((None, 2, 2, LRU_BLOCKS, LRU_BS, LRU_BS), lambda b: (layer, 0, 0, 0, 0, 0)),
            pl.BlockSpec((None, 2, 2, LRU_WIDTH), lambda b: (layer, 0, 0, 0)),
            pl.BlockSpec((None, 2, LRU_WIDTH), lambda b: (layer, 0, 0)),
            h0_spec,
        ],
        out_specs=[
            pl.BlockSpec((seq, LRU_WIDTH), lambda b: (b, 0)),
            pl.BlockSpec((None, 2, LRU_WIDTH), lambda b: (b, 0, 0)),
        ],
        out_shape=[
            jax.ShapeDtypeStruct((nb * seq, LRU_WIDTH), BF16),
            jax.ShapeDtypeStruct((nb, 2, LRU_WIDTH), F32),
        ],
        scratch_shapes=[
            pltpu.VMEM((seq + 2 * PAD, LRU_WIDTH), F32),
            pltpu.VMEM((seq, LRU_WIDTH), F32),
            pltpu.VMEM((seq, LRU_WIDTH), F32),
            pltpu.VMEM((LRU_TC, LRU_WIDTH), F32),
            pltpu.VMEM((LRU_TC, LRU_WIDTH), F32),
        ],
        compiler_params=_params(("arbitrary",)),
        name="lru_lat" if latent else "lru_ctx",
    )(uf, uf, conv_w, conv_b.reshape(DEPTH, 1, LRU_WIDTH), gate_w, gate_b, lam, h0)


OUT_TM = 512
OUT_SUB = 256


def _outproj_kernel(x_ref, ma_ref, mb_ref, mc_ref, md_ref, mod_ref, g_ref, w_ref, x1_ref, h2_ref):
    for c in range(OUT_TM // OUT_SUB):
        rows = slice(c * OUT_SUB, (c + 1) * OUT_SUB)
        y = None
        for n, m_ref in enumerate((ma_ref, mb_ref, mc_ref, md_ref)):
            part = _dot(m_ref[rows, :], w_ref[n * COL_BLOCK:(n + 1) * COL_BLOCK, :])
            y = part if y is None else y + part
        x1 = x_ref[rows, :] + mod_ref[2:3, :] * (_rms(y) * g_ref[1:2, :])
        x1_ref[rows, :] = x1
        h2 = _rms(x1) * g_ref[2:3, :]
        h2_ref[rows, :] = (h2 * (1.0 + mod_ref[4:5, :]) + mod_ref[3:4, :]).astype(BF16)


def _outproj_call(x, parts, mod, norm_g, w_out_b, layer, *, latent):
    rows = x.shape[0]
    tm = OUT_TM
    part_spec = pl.BlockSpec((tm, COL_BLOCK), lambda i: (i, 0))
    return pl.pallas_call(
        _outproj_kernel,
        grid=(rows // tm,),
        in_specs=[
            pl.BlockSpec((tm, D_MODEL), lambda i: (i, 0)),
            part_spec, part_spec, part_spec, part_spec,
            _mod_spec(layer, latent, tm, 1),
            pl.BlockSpec((None, 4, D_MODEL), lambda i: (layer, 0, 0)),
            pl.BlockSpec((None, MIX_WIDTH, D_MODEL), lambda i: (layer, 0, 0), pipeline_mode=RESIDENT),
        ],
        out_specs=[
            pl.BlockSpec((tm, D_MODEL), lambda i: (i, 0)),
            pl.BlockSpec((tm, D_MODEL), lambda i: (i, 0)),
        ],
        out_shape=[
            jax.ShapeDtypeStruct((rows, D_MODEL), F32),
            jax.ShapeDtypeStruct((rows, D_MODEL), BF16),
        ],
        compiler_params=_params(("arbitrary",)),
        name="outproj_lat" if latent else "outproj_ctx",
    )(x, *parts, mod, norm_g, w_out_b)


FFN_TM = 512
FFN_TH = 512
FFN_SUB = 256


def _ffn_kernel(h_ref, x1_ref, mod_ref, g_ref, wa_ref, wg_ref, wo_ref, o_ref, acc_scr):
    j = pl.program_id(1)

    @pl.when(j == 0)
    def _():
        acc_scr[...] = jnp.zeros_like(acc_scr)

    h = h_ref[...]
    c = None
    for s in range(FFN_TH // FFN_SUB):
        sl = slice(s * FFN_SUB, (s + 1) * FFN_SUB)
        a = _dot(h, wa_ref[:, sl])
        g = _dot(h, wg_ref[:, sl])
        t = ((g * jax.nn.sigmoid(g)) * a).astype(BF16)
        cs = _dot(t, wo_ref[sl, :])
        c = cs if c is None else c + cs
    acc_scr[...] += c

    @pl.when(j == pl.num_programs(1) - 1)
    def _():
        o_ref[...] = x1_ref[...] + mod_ref[5:6, :] * (_rms(acc_scr[...]) * g_ref[3:4, :])


def _ffn_call(h2, x1, mod, norm_g, w_ffn_in_b, w_ffn_out_b, layer, *, latent):
    rows = x1.shape[0]
    tm, th = FFN_TM, FFN_TH
    nh = FFN_HIDDEN // th
    return pl.pallas_call(
        _ffn_kernel,
        grid=(rows // tm, nh),
        in_specs=[
            pl.BlockSpec((tm, D_MODEL), lambda i, j: (i, 0)),
            pl.BlockSpec((tm, D_MODEL), lambda i, j: (i, 0)),
            _mod_spec(layer, latent, tm, 2),
            pl.BlockSpec((None, 4, D_MODEL), lambda i, j: (layer, 0, 0)),
            pl.BlockSpec((None, D_MODEL, th), lambda i, j: (layer, 0, j)),
            pl.BlockSpec((None, D_MODEL, th), lambda i, j: (layer, 0, nh + j)),
            pl.BlockSpec((None, th, D_MODEL), lambda i, j: (layer, j, 0)),
        ],
        out_specs=pl.BlockSpec((tm, D_MODEL), lambda i, j: (i, 0)),
        out_shape=jax.ShapeDtypeStruct((rows, D_MODEL), F32),
        scratch_shapes=[pltpu.VMEM((tm, D_MODEL), F32)],
        compiler_params=_params(("arbitrary", "arbitrary")),
        name="ffn_lat" if latent else "ffn_ctx",
    )(h2, x1, mod, norm_g, w_ffn_in_b, w_ffn_in_b, w_ffn_out_b)


def kernel(x_prompt, x_sample, cache_diff_kv, state_rglru, cache_na_kv, cache_gqa_kv, c, c_ctx, w_mod, b_mod,
           norm_g, w_in, diff_lambda_w, lru_conv_w, lru_conv_b, lru_gate_w, lru_gate_b, lru_lambda, na_rpb,
           gqa_qk_g, w_out, w_ffn_in, w_ffn_out):
    cond = jnp.concatenate([c_ctx[None, :], c, jnp.zeros((MOD_ROWS - 1 - DEC_BATCH, D_MODEL), F32)], axis=0)
    mod = _mod_call(cond, w_mod, b_mod).reshape(DEPTH, MOD_ROWS, 6, D_MODEL)

    w_in_b = w_in.astype(BF16)
    w_out_b = w_out.astype(BF16)
    w_ffn_in_b = w_ffn_in.astype(BF16)
    w_ffn_out_b = w_ffn_out.astype(BF16)
    rope_tabs = _rope_tables()
    na_bias = _na_bias_tables(na_rpb)

    cache_a = cache_diff_kv.reshape(DEC_BATCH, DEPTH, 2, PAST_LEN, DA_HEADS * HEAD_DIM)
    cache_c = cache_na_kv.reshape(DEC_BATCH, DEPTH, 2, PAST_LEN, NA_HEADS * HEAD_DIM)
    cache_d = cache_gqa_kv.reshape(DEC_BATCH, DEPTH, 2, PAST_LEN, GQA_KV_HEADS * HEAD_DIM)
    zero_state = jnp.zeros((BATCH, 2, LRU_WIDTH), F32)

    def layer(x, l, latent, new_caches):
        outs = _inproj_call(x, mod, norm_g, w_in_b, gqa_qk_g, rope_tabs, new_caches, l, latent=latent)
        ub, uf = outs[0], outs[1]
        m_b, st = _lru_call(uf, lru_conv_w, lru_conv_b, lru_gate_w, lru_gate_b, lru_lambda,
                            state_rglru if latent else zero_state, l, latent=latent)
        if latent:
            m_a = _diff_call(ub, diff_lambda_w, cache_a, l)
            m_c = _na_call(ub, cache_c, na_bias, l)
            m_d = _gqa_call(ub, cache_d, l)
        else:
            m_a, m_c, m_d = _attn_ctx_call(ub, diff_lambda_w, l)
        x1, h2 = _outproj_call(x, (m_a, m_b, m_c, m_d), mod, norm_g, w_out_b, l, latent=latent)
        x_new = _ffn_call(h2, x1, mod, norm_g, w_ffn_in_b, w_ffn_out_b, l, latent=latent)
        return x_new, tuple(outs[2:]), st

    y = x_prompt.reshape(BATCH * SEQ, D_MODEL)
    new_caches = None
    states = []
    for l in range(DEPTH):
        y, new_caches, st = layer(y, l, False, new_caches)
        states.append(st)
    y_prompt = y.reshape(BATCH, SEQ, D_MODEL)
    new_cache_diff_kv = new_caches[0].reshape(BATCH, DEPTH, 2, SEQ, DA_HEADS, HEAD_DIM)
    new_state_rglru = jnp.stack(states, axis=1)
    new_cache_na_kv = new_caches[1].reshape(BATCH, DEPTH, 2, SEQ, NA_HEADS, HEAD_DIM)
    new_cache_gqa_kv = new_caches[2].reshape(BATCH, DEPTH, 2, SEQ, GQA_KV_HEADS, HEAD_DIM)

    z = x_sample.reshape(DEC_BATCH * DEC_SEQ, D_MODEL)
    for l in range(DEPTH):
        z, _, _ = layer(z, l, True, None)
    y_sample = z.reshape(DEC_BATCH, DEC_SEQ, D_MODEL)

    return (y_prompt, y_sample, new_cache_diff_kv, new_state_rglru, new_cache_na_kv, new_cache_gqa_kv)
```

```python
import functools
import math

import jax
import jax.numpy as jnp
import numpy as np
from jax import lax
from jax.experimental import pallas as pl
from jax.experimental.pallas import tpu as pltpu

D_MODEL = 2048
BATCH = 16
SEQ = 256
DEPTH = 4
DEC_BATCH = 4
DEC_SEQ = 2048
PAST_LEN = 512
GRID_W = 64
GRID_H = DEC_SEQ // GRID_W
HEAD_DIM = 128
EPS = 1e-6
ROPE_THETA = 10000.0
DA_HEADS = 4
DA_DH = HEAD_DIM // 2
LRU_WIDTH = 512
LRU_BLOCKS = 4
LRU_BS = LRU_WIDTH // LRU_BLOCKS
LRU_C = 8.0
CONV_W = 4
CONV_LEFT = 2
NA_HEADS = 4
NA_ROWS = 8
NA_COLS = 16
GQA_Q_HEADS = 4
GQA_KV_HEADS = 2
IN_WIDTH = 5120
MIX_WIDTH = 2048
FFN_HIDDEN = 5632
N_MOD = 6 * D_MODEL

F32 = jnp.float32
BF16 = jnp.bfloat16

VMEM_LIMIT_BYTES = 56 * 1024 * 1024
LANES = 128
LOG2E = math.log2(math.e)

COL_AQ, COL_AK, COL_AV, COL_BX, COL_BG, COL_CQ, COL_CK, COL_CV, COL_DQ, COL_DKV = range(10)
N_COL_BLOCKS = 10
COL_BLOCK = 512
HEADS_PER_BLOCK = COL_BLOCK // LANES

NA_QROWS = 4
NA_WIN_ROWS = NA_QROWS + NA_ROWS
NA_TQ = NA_QROWS * GRID_W
NA_WIN = NA_WIN_ROWS * GRID_W
NEG_BIG = -1e30

RESIDENT = pl.Buffered(1)


def _params(sem):
    return pltpu.CompilerParams(dimension_semantics=sem, vmem_limit_bytes=VMEM_LIMIT_BYTES)


def _rms(x):
    return x * lax.rsqrt(jnp.mean(x * x, axis=-1, keepdims=True) + EPS)


def _dot(a, b):
    return jnp.dot(a, b, preferred_element_type=F32)


def _dot_nt(a, b):
    return lax.dot_general(a, b, (((1,), (1,)), ((), ())), preferred_element_type=F32)


MOD_ROWS = 8
MOD_TN = 1024


def _mod_kernel(cond_ref, w_ref, b_ref, o_ref):
    c = cond_ref[...]
    s = (c * jax.nn.sigmoid(c)).astype(BF16)
    o_ref[...] = _dot(s, w_ref[...].astype(BF16)) + b_ref[...]


def _mod_call(cond, w_mod, b_mod):
    return pl.pallas_call(
        _mod_kernel,
        grid=(DEPTH, N_MOD // MOD_TN),
        in_specs=[
            pl.BlockSpec((MOD_ROWS, D_MODEL), lambda l, n: (0, 0)),
            pl.BlockSpec((None, D_MODEL, MOD_TN), lambda l, n: (l, 0, n)),
            pl.BlockSpec((None, 1, MOD_TN), lambda l, n: (l, 0, n)),
        ],
        out_specs=pl.BlockSpec((None, MOD_ROWS, MOD_TN), lambda l, n: (l, 0, n)),
        out_shape=jax.ShapeDtypeStruct((DEPTH, MOD_ROWS, N_MOD), F32),
        compiler_params=_params(("arbitrary", "arbitrary")),
        name="mod",
    )(cond, w_mod, b_mod.reshape(DEPTH, 1, N_MOD))


def _mod_spec(layer, latent, tm, ngrid):
    if latent:
        per_batch = DEC_SEQ // tm
        idx = lambda i: 1 + i // per_batch
    else:
        idx = lambda i: 0
    if ngrid == 1:
        return pl.BlockSpec((None, None, 6, D_MODEL), lambda i: (layer, idx(i), 0, 0))
    return pl.BlockSpec((None, None, 6, D_MODEL), lambda i, j: (layer, idx(i), 0, 0))


IN_TM = 256


def _rope(seg, cos, sin_lo, sin_hi, off):
    return (seg * cos + pltpu.roll(seg, LANES - off, 1) * sin_lo + pltpu.roll(seg, off, 1) * sin_hi)


def _inproj_kernel(*refs, latent, nbt):
    if latent:
        (x_ref, mod_ref, g_ref, w_ref, qkg_ref, ra_ref, rd_ref, ub_ref, uf_ref, h_scr) = refs
        ca_ref = cc_ref = cd_ref = None
    else:
        (x_ref, mod_ref, g_ref, w_ref, qkg_ref, _, _, _, ub_ref, uf_ref, ca_ref, cc_ref, cd_ref, h_scr) = refs

    y = _rms(x_ref[...]) * g_ref[0:1, :]
    h_scr[...] = (y * (1.0 + mod_ref[1:2, :]) + mod_ref[0:1, :]).astype(BF16)

    def rope_a(seg):
        return _rope(seg, ra_ref[0], ra_ref[1], ra_ref[2], DA_DH // 4) if latent else seg

    def rope_d(seg):
        return _rope(seg, rd_ref[0], rd_ref[1], rd_ref[2], HEAD_DIM // 4) if latent else seg

    def to_cache(c_ref, kv, lane0, val):
        if latent:
            return
        for bb in range(nbt):
            c_ref[bb, kv, :, lane0:lane0 + LANES] = val[bb * SEQ:(bb + 1) * SEQ]

    for col in range(N_COL_BLOCKS):
        c0 = col * COL_BLOCK
        acc = _dot(h_scr[...], w_ref[:, c0:c0 + COL_BLOCK])
        if col in (COL_BX, COL_BG):
            uf_ref[:, (col - COL_BX) * COL_BLOCK:(col - COL_BX + 1) * COL_BLOCK] = acc
            ub_ref[:, c0:c0 + COL_BLOCK] = acc.astype(BF16)
            continue
        for hh in range(HEADS_PER_BLOCK):
            l0 = hh * LANES
            seg = acc[:, l0:l0 + LANES]
            if col == COL_AQ:
                seg = rope_a(seg) * (DA_DH ** -0.5 * LOG2E)
            elif col == COL_AK:
                seg = rope_a(seg)
            elif col == COL_CQ:
                seg = seg * (HEAD_DIM ** -0.5 * LOG2E)
            elif col == COL_DQ:
                seg = rope_d(_rms(seg) * qkg_ref[0:1, :]) * (HEAD_DIM ** -0.5 * LOG2E)
            elif col == COL_DKV and hh < GQA_KV_HEADS:
                seg = rope_d(_rms(seg) * qkg_ref[1:2, :])
            ub_ref[:, c0 + l0:c0 + l0 + LANES] = seg.astype(BF16)
            if col in (COL_AK, COL_AV):
                to_cache(ca_ref, col - COL_AK, l0, seg)
            elif col in (COL_CK, COL_CV):
                to_cache(cc_ref, col - COL_CK, l0, seg)
            elif col == COL_DKV:
                to_cache(cd_ref, hh // GQA_KV_HEADS, (hh % GQA_KV_HEADS) * LANES, seg)


def _inproj_call(x, mod, norm_g, w_in_b, qk_g, rope_tabs, caches, layer, *, latent):
    rows = x.shape[0]
    tm = IN_TM
    in_specs = [
        pl.BlockSpec((tm, D_MODEL), lambda i: (i, 0)),
        _mod_spec(layer, latent, tm, 1),
        pl.BlockSpec((None, 4, D_MODEL), lambda i: (layer, 0, 0)),
        pl.BlockSpec((None, D_MODEL, IN_WIDTH), lambda i: (layer, 0, 0), pipeline_mode=RESIDENT),
        pl.BlockSpec((None, 2, HEAD_DIM), lambda i: (layer, 0, 0)),
    ]
    args = [x, mod, norm_g, w_in_b, qk_g]
    out_specs = [
        pl.BlockSpec((tm, IN_WIDTH), lambda i: (i, 0)),
        pl.BlockSpec((tm, 2 * COL_BLOCK), lambda i: (i, 0)),
    ]
    out_shape = [
        jax.ShapeDtypeStruct((rows, IN_WIDTH), BF16),
        jax.ShapeDtypeStruct((rows, 2 * COL_BLOCK), F32),
    ]
    aliases = {}
    nbt = 1
    if latent:
        per_batch = DEC_SEQ // tm
        tab_spec = pl.BlockSpec((3, tm, LANES), lambda i: (0, i % per_batch, 0))
        in_specs += [tab_spec, tab_spec]
        args += [rope_tabs[0], rope_tabs[1]]
    else:
        nbt = tm // SEQ
        for n, cache in enumerate(caches):
            out_specs.append(pl.BlockSpec((nbt, None, 2, SEQ, cache.shape[-1]), lambda i: (i, layer, 0, 0, 0)))
            out_shape.append(jax.ShapeDtypeStruct(cache.shape, F32))
            in_specs.append(pl.BlockSpec(memory_space=pl.ANY))
            args.append(cache)
            aliases[len(args) - 1] = 2 + n
    return pl.pallas_call(
        functools.partial(_inproj_kernel, latent=latent, nbt=nbt),
        grid=(rows // tm,),
        in_specs=in_specs,
        out_specs=out_specs,
        out_shape=out_shape,
        input_output_aliases=aliases,
        scratch_shapes=[pltpu.VMEM((tm, D_MODEL), BF16)],
        compiler_params=_params(("arbitrary",)),
        name="inproj_lat" if latent else "inproj_ctx",
    )(*args)


def _rope_tables():
    t = jnp.arange(DEC_SEQ)
    row = (t // GRID_W).astype(F32)[:, None]
    col = (t % GRID_W).astype(F32)[:, None]
    lane = np.arange(LANES)

    def table(d):
        m = d // 2
        freqs = ROPE_THETA ** (-jnp.arange(0, m, 2, dtype=F32) / m)
        within = lane % m
        k = within % (m // 2)
        hi = (within >= m // 2)[None, :]
        use_col = ((lane % d) >= m)[None, :]
        ang = jnp.where(use_col, col, row) * freqs[k][None, :]
        cos, sin = jnp.cos(ang), jnp.sin(ang)
        return jnp.stack([cos, jnp.where(hi, 0.0, -sin), jnp.where(hi, sin, 0.0)])

    return table(DA_DH), table(HEAD_DIM)


def _with_ones(v):
    return jnp.concatenate([v, jnp.ones_like(v)], axis=-1)


def _softmax_av(q, kt, v_ones):
    s = _dot(q, kt)
    e = jnp.exp2(s - s.max(axis=-1, keepdims=True)).astype(BF16)
    acc = _dot(e, v_ones)
    return acc[:, :LANES] * (1.0 / acc[:, LANES:])


def _fill_kv(kt_scr, v_scr, k_ref, v_ref, kc_ref, vc_ref):
    n_new = k_ref.shape[0]
    p = kc_ref.shape[0]
    kt_scr[:, 0:p] = kc_ref[...].T.astype(BF16)
    kt_scr[:, p:p + n_new] = k_ref[...].astype(F32).T.astype(BF16)
    v_scr[0:p, 0:LANES] = vc_ref[...].astype(BF16)
    v_scr[p:p + n_new, 0:LANES] = v_ref[...]
    v_scr[:, LANES:] = jnp.ones((p + n_new, LANES), BF16)


ATT_TQ = 2048
ATT_SUB = 256
LAT_KEYS = PAST_LEN + DEC_SEQ


def _diff_lambda(lamw_ref, lam_init):
    lw = lamw_ref[...]
    return (jnp.exp(jnp.sum(lw[0:1] * lw[1:2], axis=-1, keepdims=True))
            - jnp.exp(jnp.sum(lw[2:3] * lw[3:4], axis=-1, keepdims=True)) + lam_init)


def _diff_head(q, kt, v_ones, lam, lam_init):
    r = q.shape[0]
    lane = lax.broadcasted_iota(jnp.int32, q.shape, 1)
    zero = jnp.zeros_like(q)
    qs = jnp.concatenate([jnp.where(lane < DA_DH, q, zero), jnp.where(lane >= DA_DH, q, zero)], axis=0)
    o = _softmax_av(qs, kt, v_ones)
    return _rms(o[:r] - lam * o[r:]) * (1.0 - lam_init)


def _diff_kernel(lamw_ref, q_ref, k_ref, v_ref, kc_ref, vc_ref, o_ref, kt_scr, v_scr, *, lam_init):
    @pl.when(pl.program_id(2) == 0)
    def _():
        _fill_kv(kt_scr, v_scr, k_ref, v_ref, kc_ref, vc_ref)

    lam = _diff_lambda(lamw_ref, lam_init)
    for c in range(ATT_TQ // ATT_SUB):
        rows = slice(c * ATT_SUB, (c + 1) * ATT_SUB)
        o_ref[rows, :] = _diff_head(q_ref[rows, :], kt_scr[...], v_scr[...], lam, lam_init).astype(BF16)


def _cache_specs(layer, head_of):
    return [pl.BlockSpec((None, None, None, PAST_LEN, LANES),
                         lambda b, h, i, kv=kv: (b, layer, kv, 0, head_of(h))) for kv in (0, 1)]


def _kv_scratch():
    return [pltpu.VMEM((HEAD_DIM, LAT_KEYS), BF16), pltpu.VMEM((LAT_KEYS, 2 * LANES), BF16)]


def _diff_call(ub, lam_w, cache, layer):
    tq = ATT_TQ
    nq = DEC_SEQ // tq
    lam_init = 0.8 - 0.6 * math.exp(-0.3 * layer)
    return pl.pallas_call(
        functools.partial(_diff_kernel, lam_init=lam_init),
        grid=(DEC_BATCH, DA_HEADS, nq),
        in_specs=[
            pl.BlockSpec((None, 4, DA_DH), lambda b, h, i: (layer, 0, 0)),
            pl.BlockSpec((tq, LANES), lambda b, h, i: (b * nq + i, COL_AQ * 4 + h)),
            pl.BlockSpec((DEC_SEQ, LANES), lambda b, h, i: (b, COL_AK * 4 + h)),
            pl.BlockSpec((DEC_SEQ, LANES), lambda b, h, i: (b, COL_AV * 4 + h)),
            *_cache_specs(layer, lambda h: h),
        ],
        out_specs=pl.BlockSpec((tq, LANES), lambda b, h, i: (b * nq + i, h)),
        out_shape=jax.ShapeDtypeStruct((DEC_BATCH * DEC_SEQ, DA_HEADS * HEAD_DIM), BF16),
        scratch_shapes=_kv_scratch(),
        compiler_params=_params(("arbitrary", "arbitrary", "arbitrary")),
        name="diff_lat",
    )(lam_w, ub, ub, ub, cache, cache)


GQA_GROUP = GQA_Q_HEADS // GQA_KV_HEADS


def _gqa_kernel(q_ref, k_ref, v_ref, kc_ref, vc_ref, o_ref, kt_scr, v_scr):
    @pl.when(pl.program_id(2) == 0)
    def _():
        _fill_kv(kt_scr, v_scr, k_ref, v_ref, kc_ref, vc_ref)

    for c in range(ATT_TQ // ATT_SUB):
        r0 = c * ATT_SUB
        qs = jnp.concatenate([q_ref[r0:r0 + ATT_SUB, g * LANES:(g + 1) * LANES] for g in range(GQA_GROUP)], axis=0)
        o = _softmax_av(qs, kt_scr[...], v_scr[...])
        for g in range(GQA_GROUP):
            o_ref[r0:r0 + ATT_SUB, g * LANES:(g + 1) * LANES] = o[g * ATT_SUB:(g + 1) * ATT_SUB].astype(BF16)


def _gqa_call(ub, cache, layer):
    tq = ATT_TQ
    nq = DEC_SEQ // tq
    qw = GQA_GROUP * LANES
    return pl.pallas_call(
        _gqa_kernel,
        grid=(DEC_BATCH, GQA_KV_HEADS, nq),
        in_specs=[
            pl.BlockSpec((tq, qw), lambda b, n, i: (b * nq + i, COL_DQ * 4 // GQA_GROUP + n)),
            pl.BlockSpec((DEC_SEQ, LANES), lambda b, n, i: (b, COL_DKV * 4 + n)),
            pl.BlockSpec((DEC_SEQ, LANES), lambda b, n, i: (b, COL_DKV * 4 + GQA_KV_HEADS + n)),
            *_cache_specs(layer, lambda n: n),
        ],
        out_specs=pl.BlockSpec((tq, qw), lambda b, n, i: (b * nq + i, n)),
        out_shape=jax.ShapeDtypeStruct((DEC_BATCH * DEC_SEQ, GQA_Q_HEADS * HEAD_DIM), BF16),
        scratch_shapes=_kv_scratch(),
        compiler_params=_params(("arbitrary", "arbitrary", "arbitrary")),
        name="attn_d_lat",
    )(ub, ub, ub, cache, cache)


def _attn_ctx_kernel(lamw_ref, u_ref, oa_ref, oc_ref, od_ref, *, lam_init):
    def head(col, hh):
        c0 = col * COL_BLOCK + hh * LANES
        return u_ref[:, c0:c0 + LANES]

    def keys_t(col, hh):
        return head(col, hh).astype(F32).T.astype(BF16)

    lam = _diff_lambda(lamw_ref, lam_init)
    for hh in range(DA_HEADS):
        o = _diff_head(head(COL_AQ, hh), keys_t(COL_AK, hh), _with_ones(head(COL_AV, hh)), lam, lam_init)
        oa_ref[:, hh * LANES:(hh + 1) * LANES] = o.astype(BF16)
    for hh in range(NA_HEADS):
        o = _softmax_av(head(COL_CQ, hh), keys_t(COL_CK, hh), _with_ones(head(COL_CV, hh)))
        oc_ref[:, hh * LANES:(hh + 1) * LANES] = o.astype(BF16)
    for n in range(GQA_KV_HEADS):
        qs = jnp.concatenate([head(COL_DQ, n * GQA_GROUP + g) for g in range(GQA_GROUP)], axis=0)
        o = _softmax_av(qs, keys_t(COL_DKV, n), _with_ones(head(COL_DKV, GQA_KV_HEADS + n)))
        for g in range(GQA_GROUP):
            hq = n * GQA_GROUP + g
            od_ref[:, hq * LANES:(hq + 1) * LANES] = o[g * SEQ:(g + 1) * SEQ].astype(BF16)


def _attn_ctx_call(ub, lam_w, layer):
    lam_init = 0.8 - 0.6 * math.exp(-0.3 * layer)
    out_spec = pl.BlockSpec((SEQ, COL_BLOCK), lambda b: (b, 0))
    out_sds = jax.ShapeDtypeStruct((BATCH * SEQ, COL_BLOCK), BF16)
    return pl.pallas_call(
        functools.partial(_attn_ctx_kernel, lam_init=lam_init),
        grid=(BATCH,),
        in_specs=[
            pl.BlockSpec((None, 4, DA_DH), lambda b: (layer, 0, 0)),
            pl.BlockSpec((SEQ, IN_WIDTH), lambda b: (b, 0)),
        ],
        out_specs=[out_spec, out_spec, out_spec],
        out_shape=[out_sds, out_sds, out_sds],
        compiler_params=_params(("arbitrary",)),
        name="attn_ctx",
    )(lam_w, ub)


def _na_win_row0(r0):
    return np.clip(r0 - NA_ROWS // 2, 0, GRID_H - NA_WIN_ROWS)


def _na_bias_tables(rpb):
    nd = 2 * NA_COLS - 1
    lo = (GRID_W - 1) - (NA_COLS - 1)
    period = 2 * GRID_W
    p = jnp.pad(rpb, ((0, 0), (0, 0), (0, 0), (lo, period - nd - lo)))
    tiled = jnp.tile(p, (1, 1, 1, GRID_W))[..., :GRID_W * (period - 1)]
    toe = tiled.reshape(DEPTH, NA_HEADS, 2 * NA_ROWS - 1, GRID_W, period - 1)[..., GRID_W - 1:]
    qc = np.arange(GRID_W)
    cstart = np.clip(qc - NA_COLS // 2, 0, GRID_W - NA_COLS)
    col_ok = (qc[None, :] >= cstart[:, None]) & (qc[None, :] < cstart[:, None] + NA_COLS)
    toe = jnp.where(col_ok, toe, NEG_BIG)
    masked = jnp.full((DEPTH, NA_HEADS, GRID_W, GRID_W), NEG_BIG, F32)
    cases = []
    for r0 in (0, NA_QROWS, GRID_H - NA_QROWS):
        w0 = int(_na_win_row0(r0))
        rows = []
        for qa in range(NA_QROWS):
            qr = r0 + qa
            rstart = int(np.clip(qr - NA_ROWS // 2, 0, GRID_H - NA_ROWS))
            blocks = []
            for kj in range(NA_WIN_ROWS):
                kr = w0 + kj
                inside = rstart <= kr < rstart + NA_ROWS
                blocks.append(toe[:, :, kr - qr + NA_ROWS - 1] if inside else masked)
            rows.append(jnp.concatenate(blocks, axis=-1))
        cases.append(jnp.concatenate(rows, axis=-2))
    return jnp.stack(cases, axis=2)


NA_CHAINS = 4
NA_BLOCKS = GRID_H // NA_QROWS


def _na_kernel(q_ref, k_ref, v_ref, kc_ref, vc_ref, bias_ref, o_ref, kt_scr, v_scr):
    step = pl.program_id(2)

    @pl.when(step == 0)
    def _():
        _fill_kv(kt_scr, v_scr, k_ref, v_ref, kc_ref, vc_ref)

    for c in range(NA_CHAINS):
        rb = step * NA_CHAINS + c
        rows = slice(c * NA_TQ, (c + 1) * NA_TQ)
        w0 = pl.multiple_of(
            PAST_LEN + jnp.clip(rb * NA_QROWS - NA_ROWS // 2, 0, GRID_H - NA_WIN_ROWS) * GRID_W,
            NA_QROWS * GRID_W)
        case = jnp.where(rb == 0, 0, jnp.where(rb == NA_BLOCKS - 1, 2, 1))
        q = q_ref[rows, :]
        s_loc = _dot(q, kt_scr[:, pl.ds(w0, NA_WIN)]) + bias_ref[case] * LOG2E
        s_ctx = _dot(q, kt_scr[:, 0:PAST_LEN])
        m = jnp.maximum(s_loc.max(axis=-1, keepdims=True), s_ctx.max(axis=-1, keepdims=True))
        acc = (_dot(jnp.exp2(s_loc - m).astype(BF16), v_scr[pl.ds(w0, NA_WIN), :])
               + _dot(jnp.exp2(s_ctx - m).astype(BF16), v_scr[0:PAST_LEN, :]))
        o_ref[rows, :] = (acc[:, :LANES] * (1.0 / acc[:, LANES:])).astype(BF16)


def _na_call(ub, cache, bias_tabs, layer):
    nsteps = NA_BLOCKS // NA_CHAINS
    tq = NA_CHAINS * NA_TQ
    return pl.pallas_call(
        _na_kernel,
        grid=(DEC_BATCH, NA_HEADS, nsteps),
        in_specs=[
            pl.BlockSpec((tq, LANES), lambda b, h, r: (b * nsteps + r, COL_CQ * 4 + h)),
            pl.BlockSpec((DEC_SEQ, LANES), lambda b, h, r: (b, COL_CK * 4 + h)),
            pl.BlockSpec((DEC_SEQ, LANES), lambda b, h, r: (b, COL_CV * 4 + h)),
            *_cache_specs(layer, lambda h: h),
            pl.BlockSpec((None, None, 3, NA_TQ, NA_WIN), lambda b, h, r: (layer, h, 0, 0, 0)),
        ],
        out_specs=pl.BlockSpec((tq, LANES), lambda b, h, r: (b * nsteps + r, h)),
        out_shape=jax.ShapeDtypeStruct((DEC_BATCH * DEC_SEQ, NA_HEADS * HEAD_DIM), BF16),
        scratch_shapes=_kv_scratch(),
        compiler_params=_params(("arbitrary", "arbitrary", "arbitrary")),
        name="na_lat",
    )(ub, ub, ub, cache, cache, bias_tabs)


LRU_TC = 256
SUB = 8
PAD = 8


def _gelu_tanh(x):
    return x * (0.5 * (1.0 + jnp.tanh(math.sqrt(2.0 / math.pi) * (x + 0.044715 * (x * x * x)))))


def _lru_kernel(x_ref, g_ref, cw_ref, cb_ref, gw_ref, gb_ref, lam_ref, h0_ref,
                y_ref, st_ref, xp_scr, xc_scr, hf_scr, a_scr, b_scr, *, seq):
    n_chunks = seq // LRU_TC
    zeros_pad = jnp.zeros((PAD, LRU_WIDTH), F32)
    xp_scr[0:PAD, :] = zeros_pad
    xp_scr[PAD + seq:2 * PAD + seq, :] = zeros_pad
    xp_scr[PAD:PAD + seq, :] = x_ref[...]

    rowid = lax.broadcasted_iota(jnp.int32, (SUB, LRU_WIDTH), 0)

    def coeffs(xc, d):
        xcb = xc.astype(BF16)
        pre = []
        for gi in range(2):
            parts = [_dot(xcb[:, n * LRU_BS:(n + 1) * LRU_BS], gw_ref[d, gi, n].astype(BF16))
                     for n in range(LRU_BLOCKS)]
            pre.append(jnp.concatenate(parts, axis=-1) + gb_ref[d, gi:gi + 1, :])
        r = jax.nn.sigmoid(pre[0])
        i = jax.nn.sigmoid(pre[1])
        log_a = -LRU_C * r * jax.nn.softplus(-lam_ref[d:d + 1, :])
        a = jnp.exp(log_a)
        mult = jnp.sqrt(-jnp.tanh(log_a) * (a * a + 1.0))
        return a, mult * i * xc

    def scan_tile(a, b, carry, reverse):
        for d in (1, 2, 4):
            if reverse:
                keep = rowid < SUB - d
                sh = SUB - d
            else:
                keep = rowid >= d
                sh = d
            a_s = jnp.where(keep, pltpu.roll(a, sh, 0), 1.0)
            b_s = jnp.where(keep, pltpu.roll(b, sh, 0), 0.0)
            b = a * b_s + b
            a = a * a_s
        return a * carry + b

    carry = h0_ref[0:1, :]
    for c in range(n_chunks):
        r0 = c * LRU_TC
        xc = cb_ref[...] + sum(
            xp_scr[PAD + r0 + j - CONV_LEFT:PAD + r0 + j - CONV_LEFT + LRU_TC, :] * cw_ref[j:j + 1, :]
            for j in range(CONV_W))
        xc_scr[r0:r0 + LRU_TC, :] = xc
        a, b = coeffs(xc, 0)
        a_scr[...] = a
        b_scr[...] = b

        def body(t, carry, r0=r0):
            r = pl.multiple_of(t * SUB, SUB)
            h = scan_tile(a_scr[pl.ds(r, SUB), :], b_scr[pl.ds(r, SUB), :], carry, False)
            hf_scr[pl.ds(r0 + r, SUB), :] = h
            return h[SUB - 1:SUB, :]

        carry = lax.fori_loop(0, LRU_TC // SUB, body, carry)
    st_ref[0:1, :] = carry

    carry = h0_ref[1:2, :]
    for c in reversed(range(n_chunks)):
        r0 = c * LRU_TC
        a, b = coeffs(xc_scr[r0:r0 + LRU_TC, :], 1)
        a_scr[...] = a
        b_scr[...] = b

        def body(tt, carry):
            r = pl.multiple_of((LRU_TC // SUB - 1 - tt) * SUB, SUB)
            h = scan_tile(a_scr[pl.ds(r, SUB), :], b_scr[pl.ds(r, SUB), :], carry, True)
            b_scr[pl.ds(r, SUB), :] = h
            return h[0:1, :]

        carry = lax.fori_loop(0, LRU_TC // SUB, body, carry)
        y = (hf_scr[r0:r0 + LRU_TC, :] + b_scr[...]) * _gelu_tanh(g_ref[r0:r0 + LRU_TC, :])
        y_ref[r0:r0 + LRU_TC, :] = y.astype(BF16)
    st_ref[1:2, :] = carry


def _lru_call(uf, conv_w, conv_b, gate_w, gate_b, lam, h0, layer, *, latent):
    seq = DEC_SEQ if latent else SEQ
    nb = DEC_BATCH if latent else BATCH
    if latent:
        h0_spec = pl.BlockSpec((None, None, 2, LRU_WIDTH), lambda b: (b, layer, 0, 0))
    else:
        h0_spec = pl.BlockSpec((None, 2, LRU_WIDTH), lambda b: (b, 0, 0))
    return pl.pallas_call(
        functools.partial(_lru_kernel, seq=seq),
        grid=(nb,),
        in_specs=[
            pl.BlockSpec((seq, LRU_WIDTH), lambda b: (b, 0)),
            pl.BlockSpec((seq, LRU_WIDTH), lambda b: (b, 1)),
            pl.BlockSpec((None, CONV_W, LRU_WIDTH), lambda b: (layer, 0, 0)),
            pl.BlockSpec((None, 1, LRU_WIDTH), lambda b: (layer, 0, 0)),
            pl.BlockSpec((None, 2, 2, LRU_BLOCKS, LRU_BS, LRU_BS), lambda b: (layer, 0, 0, 0, 0, 0)),
            pl.BlockSpec((None, 2, 2, LRU_WIDTH), lambda b: (layer, 0, 0, 0)),
            pl.BlockSpec((None, 2, LRU_WIDTH), lambda b: (layer, 0, 0)),
            h0_spec,
        ],
        out_specs=[
            pl.BlockSpec((seq, LRU_WIDTH), lambda b: (b, 0)),
            pl.BlockSpec((None, 2, LRU_WIDTH), lambda b: (b, 0, 0)),
        ],
        out_shape=[
            jax.ShapeDtypeStruct((nb * seq, LRU_WIDTH), BF16),
            jax.ShapeDtypeStruct((nb, 2, LRU_WIDTH), F32),
        ],
        scratch_shapes=[
            pltpu.VMEM((seq + 2 * PAD, LRU_WIDTH), F32),
            pltpu.VMEM((seq, LRU_WIDTH), F32),
            pltpu.VMEM((seq, LRU_WIDTH), F32),
            pltpu.VMEM((LRU_TC, LRU_WIDTH), F32),
            pltpu.VMEM((LRU_TC, LRU_WIDTH), F32),
        ],
        compiler_params=_params(("arbitrary",)),
        name="lru_lat" if latent else "lru_ctx",
    )(uf, uf, conv_w, conv_b.reshape(DEPTH, 1, LRU_WIDTH), gate_w, gate_b, lam, h0)


OUT_TM = 512
OUT_SUB = 256


def _outproj_kernel(x_ref, ma_ref, mb_ref, mc_ref, md_ref, mod_ref, g_ref, w_ref, x1_ref, h2_ref):
    for c in range(OUT_TM // OUT_SUB):
        rows = slice(c * OUT_SUB, (c + 1) * OUT_SUB)
        y = None
        for n, m_ref in enumerate((ma_ref, mb_ref, mc_ref, md_ref)):
            part = _dot(m_ref[rows, :], w_ref[n * COL_BLOCK:(n + 1) * COL_BLOCK, :])
            y = part if y is None else y + part
        x1 = x_ref[rows, :] + mod_ref[2:3, :] * (_rms(y) * g_ref[1:2, :])
        x1_ref[rows, :] = x1
        h2 = _rms(x1) * g_ref[2:3, :]
        h2_ref[rows, :] = (h2 * (1.0 + mod_ref[4:5, :]) + mod_ref[3:4, :]).astype(BF16)


def _outproj_call(x, parts, mod, norm_g, w_out_b, layer, *, latent):
    rows = x.shape[0]
    tm = OUT_TM
    part_spec = pl.BlockSpec((tm, COL_BLOCK), lambda i: (i, 0))
    return pl.pallas_call(
        _outproj_kernel,
        grid=(rows // tm,),
        in_specs=[
            pl.BlockSpec((tm, D_MODEL), lambda i: (i, 0)),
            part_spec, part_spec, part_spec, part_spec,
            _mod_spec(layer, latent, tm, 1),
            pl.BlockSpec((None, 4, D_MODEL), lambda i: (layer, 0, 0)),
            pl.BlockSpec((None, MIX_WIDTH, D_MODEL), lambda i: (layer, 0, 0), pipeline_mode=RESIDENT),
        ],
        out_specs=[
            pl.BlockSpec((tm, D_MODEL), lambda i: (i, 0)),
            pl.BlockSpec((tm, D_MODEL), lambda i: (i, 0)),
        ],
        out_shape=[
            jax.ShapeDtypeStruct((rows, D_MODEL), F32),
            jax.ShapeDtypeStruct((rows, D_MODEL), BF16),
        ],
        compiler_params=_params(("arbitrary",)),
        name="outproj_lat" if latent else "outproj_ctx",
    )(x, *parts, mod, norm_g, w_out_b)


FFN_TM = 512
FFN_OUT_CHUNK = 512
FFN_TH = 512
FFN_SUB = 256


def _ffn_kernel(h_ref, x1_ref, mod_ref, g_ref, wa_ref, wg_ref, wo_ref, o_ref):
    j = pl.program_id(1)

    @pl.when(j == 0)
    def _():
        o_ref[...] = jnp.zeros_like(o_ref)

    h = h_ref[...]
    for s in range(FFN_TH // FFN_SUB):
        sl = slice(s * FFN_SUB, (s + 1) * FFN_SUB)
        a = _dot(h, wa_ref[:, sl])
        g = _dot(h, wg_ref[:, sl])
        t = ((g * jax.nn.sigmoid(g)) * a).astype(BF16)
        for n in range(D_MODEL // FFN_OUT_CHUNK):
            cols = slice(n * FFN_OUT_CHUNK, (n + 1) * FFN_OUT_CHUNK)
            o_ref[:, cols] += _dot(t, wo_ref[sl, cols])

    @pl.when(j == pl.num_programs(1) - 1)
    def _():
        o_ref[...] = x1_ref[...] + mod_ref[5:6, :] * (_rms(o_ref[...]) * g_ref[3:4, :])


def _ffn_call(h2, x1, mod, norm_g, w_ffn_in_b, w_ffn_out_b, layer, *, latent):
    rows = x1.shape[0]
    tm, th = FFN_TM, FFN_TH
    nh = FFN_HIDDEN // th
    return pl.pallas_call(
        _ffn_kernel,
        grid=(rows // tm, nh),
        in_specs=[
            pl.BlockSpec((tm, D_MODEL), lambda i, j: (i, 0)),
            pl.BlockSpec((tm, D_MODEL), lambda i, j: (i, 0)),
            _mod_spec(layer, latent, tm, 2),
            pl.BlockSpec((None, 4, D_MODEL), lambda i, j: (layer, 0, 0)),
            pl.BlockSpec((None, D_MODEL, th), lambda i, j: (layer, 0, j)),
            pl.BlockSpec((None, D_MODEL, th), lambda i, j: (layer, 0, nh + j)),
            pl.BlockSpec((None, th, D_MODEL), lambda i, j: (layer, j, 0)),
        ],
        out_specs=pl.BlockSpec((tm, D_MODEL), lambda i, j: (i, 0)),
        out_shape=jax.ShapeDtypeStruct((rows, D_MODEL), F32),
        compiler_params=_params(("arbitrary", "arbitrary")),
        name="ffn_lat" if latent else "ffn_ctx",
    )(h2, x1, mod, norm_g, w_ffn_in_b, w_ffn_in_b, w_ffn_out_b)


def kernel(x_prompt, x_sample, cache_diff_kv, state_rglru, cache_na_kv, cache_gqa_kv, c, c_ctx, w_mod, b_mod,
           norm_g, w_in, diff_lambda_w, lru_conv_w, lru_conv_b, lru_gate_w, lru_gate_b, lru_lambda, na_rpb,
           gqa_qk_g, w_out, w_ffn_in, w_ffn_out):
    cond = jnp.concatenate([c_ctx[None, :], c, jnp.zeros((MOD_ROWS - 1 - DEC_BATCH, D_MODEL), F32)], axis=0)
    mod = _mod_call(cond, w_mod, b_mod).reshape(DEPTH, MOD_ROWS, 6, D_MODEL)

    w_in_b = w_in.astype(BF16)
    w_out_b = w_out.astype(BF16)
    w_ffn_in_b = w_ffn_in.astype(BF16)
    w_ffn_out_b = w_ffn_out.astype(BF16)
    rope_tabs = _rope_tables()
    na_bias = _na_bias_tables(na_rpb)

    cache_a = cache_diff_kv.reshape(DEC_BATCH, DEPTH, 2, PAST_LEN, DA_HEADS * HEAD_DIM)
    cache_c = cache_na_kv.reshape(DEC_BATCH, DEPTH, 2, PAST_LEN, NA_HEADS * HEAD_DIM)
    cache_d = cache_gqa_kv.reshape(DEC_BATCH, DEPTH, 2, PAST_LEN, GQA_KV_HEADS * HEAD_DIM)
    zero_state = jnp.zeros((BATCH, 2, LRU_WIDTH), F32)

    def layer(x, l, latent, new_caches):
        outs = _inproj_call(x, mod, norm_g, w_in_b, gqa_qk_g, rope_tabs, new_caches, l, latent=latent)
        ub, uf = outs[0], outs[1]
        m_b, st = _lru_call(uf, lru_conv_w, lru_conv_b, lru_gate_w, lru_gate_b, lru_lambda,
                            state_rglru if latent else zero_state, l, latent=latent)
        if latent:
            m_a = _diff_call(ub, diff_lambda_w, cache_a, l)
            m_c = _na_call(ub, cache_c, na_bias, l)
            m_d = _gqa_call(ub, cache_d, l)
        else:
            m_a, m_c, m_d = _attn_ctx_call(ub, diff_lambda_w, l)
        x1, h2 = _outproj_call(x, (m_a, m_b, m_c, m_d), mod, norm_g, w_out_b, l, latent=latent)
        x_new = _ffn_call(h2, x1, mod, norm_g, w_ffn_in_b, w_ffn_out_b, l, latent=latent)
        return x_new, tuple(outs[2:]), st

    y = x_prompt.reshape(BATCH * SEQ, D_MODEL)
    new_caches = tuple(jnp.zeros((BATCH, DEPTH, 2, SEQ, heads * HEAD_DIM), F32)
                       for heads in (DA_HEADS, NA_HEADS, GQA_KV_HEADS))
    states = []
    for l in range(DEPTH):
        y, new_caches, st = layer(y, l, False, new_caches)
        states.append(st)
    y_prompt = y.reshape(BATCH, SEQ, D_MODEL)
    new_cache_diff_kv = new_caches[0].reshape(BATCH, DEPTH, 2, SEQ, DA_HEADS, HEAD_DIM)
    new_state_rglru = jnp.stack(states, axis=1)
    new_cache_na_kv = new_caches[1].reshape(BATCH, DEPTH, 2, SEQ, NA_HEADS, HEAD_DIM)
    new_cache_gqa_kv = new_caches[2].reshape(BATCH, DEPTH, 2, SEQ, GQA_KV_HEADS, HEAD_DIM)

    z = x_sample.reshape(DEC_BATCH * DEC_SEQ, D_MODEL)
    for l in range(DEPTH):
        z, _, _ = layer(z, l, True, None)
    y_sample = z.reshape(DEC_BATCH, DEC_SEQ, D_MODEL)

    return (y_prompt, y_sample, new_cache_diff_kv, new_state_rglru, new_cache_na_kv, new_cache_gqa_kv)
```

```python
import functools
import math

import jax
import jax.numpy as jnp
import numpy as np
from jax import lax
from jax.experimental import pallas as pl
from jax.experimental.pallas import tpu as pltpu

D_MODEL = 2048
BATCH = 16
SEQ = 256
DEPTH = 4
DEC_BATCH = 4
DEC_SEQ = 2048
PAST_LEN = 512
GRID_W = 64
GRID_H = DEC_SEQ // GRID_W
HEAD_DIM = 128
EPS = 1e-6
ROPE_THETA = 10000.0
DA_HEADS = 4
DA_DH = HEAD_DIM // 2
LRU_WIDTH = 512
LRU_BLOCKS = 4
LRU_BS = LRU_WIDTH // LRU_BLOCKS
LRU_C = 8.0
CONV_W = 4
CONV_LEFT = 2
NA_HEADS = 4
NA_ROWS = 8
NA_COLS = 16
GQA_Q_HEADS = 4
GQA_KV_HEADS = 2
IN_WIDTH = 5120
MIX_WIDTH = 2048
FFN_HIDDEN = 5632
N_MOD = 6 * D_MODEL

F32 = jnp.float32
BF16 = jnp.bfloat16

VMEM_LIMIT_BYTES = 56 * 1024 * 1024
LANES = 128
LOG2E = math.log2(math.e)

COL_AQ, COL_AK, COL_AV, COL_BX, COL_BG, COL_CQ, COL_CK, COL_CV, COL_DQ, COL_DKV = range(10)
N_COL_BLOCKS = 10
COL_BLOCK = 512
HEADS_PER_BLOCK = COL_BLOCK // LANES

NA_QROWS = 4
NA_WIN_ROWS = NA_QROWS + NA_ROWS
NA_TQ = NA_QROWS * GRID_W
NA_WIN = NA_WIN_ROWS * GRID_W
NEG_BIG = -1e30

RESIDENT = pl.Buffered(1)


def _params(sem):
    return pltpu.CompilerParams(dimension_semantics=sem, vmem_limit_bytes=VMEM_LIMIT_BYTES)


def _rms(x):
    return x * lax.rsqrt(jnp.mean(x * x, axis=-1, keepdims=True) + EPS)


def _dot(a, b):
    return jnp.dot(a, b, preferred_element_type=F32)


def _dot_nt(a, b):
    return lax.dot_general(a, b, (((1,), (1,)), ((), ())), preferred_element_type=F32)


MOD_ROWS = 8
MOD_TN = 1024


def _mod_kernel(cond_ref, w_ref, b_ref, o_ref):
    c = cond_ref[...]
    s = (c * jax.nn.sigmoid(c)).astype(BF16)
    o_ref[...] = _dot(s, w_ref[...].astype(BF16)) + b_ref[...]


def _mod_call(cond, w_mod, b_mod):
    return pl.pallas_call(
        _mod_kernel,
        grid=(DEPTH, N_MOD // MOD_TN),
        in_specs=[
            pl.BlockSpec((MOD_ROWS, D_MODEL), lambda l, n: (0, 0)),
            pl.BlockSpec((None, D_MODEL, MOD_TN), lambda l, n: (l, 0, n)),
            pl.BlockSpec((None, 1, MOD_TN), lambda l, n: (l, 0, n)),
        ],
        out_specs=pl.BlockSpec((None, MOD_ROWS, MOD_TN), lambda l, n: (l, 0, n)),
        out_shape=jax.ShapeDtypeStruct((DEPTH, MOD_ROWS, N_MOD), F32),
        compiler_params=_params(("arbitrary", "arbitrary")),
        name="mod",
    )(cond, w_mod, b_mod.reshape(DEPTH, 1, N_MOD))


def _mod_spec(layer, latent, tm, ngrid):
    if latent:
        per_batch = DEC_SEQ // tm
        idx = lambda i: 1 + i // per_batch
    else:
        idx = lambda i: 0
    if ngrid == 1:
        return pl.BlockSpec((None, None, 6, D_MODEL), lambda i: (layer, idx(i), 0, 0))
    return pl.BlockSpec((None, None, 6, D_MODEL), lambda i, j: (layer, idx(i), 0, 0))


IN_TM = 256


def _rope(seg, cos, sin_lo, sin_hi, off):
    return (seg * cos + pltpu.roll(seg, LANES - off, 1) * sin_lo + pltpu.roll(seg, off, 1) * sin_hi)


def _inproj_kernel(*refs, latent, nbt):
    if latent:
        (x_ref, mod_ref, g_ref, w_ref, qkg_ref, ra_ref, rd_ref, ub_ref, uf_ref, h_scr) = refs
        ca_ref = cc_ref = cd_ref = None
    else:
        (x_ref, mod_ref, g_ref, w_ref, qkg_ref, _, _, _, ub_ref, uf_ref, ca_ref, cc_ref, cd_ref, h_scr) = refs

    y = _rms(x_ref[...]) * g_ref[0:1, :]
    h_scr[...] = (y * (1.0 + mod_ref[1:2, :]) + mod_ref[0:1, :]).astype(BF16)

    def rope_a(seg):
        return _rope(seg, ra_ref[0], ra_ref[1], ra_ref[2], DA_DH // 4) if latent else seg

    def rope_d(seg):
        return _rope(seg, rd_ref[0], rd_ref[1], rd_ref[2], HEAD_DIM // 4) if latent else seg

    def to_cache(c_ref, kv, lane0, val):
        if latent:
            return
        for bb in range(nbt):
            c_ref[bb, kv, :, lane0 // LANES, :] = val[bb * SEQ:(bb + 1) * SEQ]

    for col in range(N_COL_BLOCKS):
        c0 = col * COL_BLOCK
        acc = _dot(h_scr[...], w_ref[:, c0:c0 + COL_BLOCK])
        if col in (COL_BX, COL_BG):
            uf_ref[:, (col - COL_BX) * COL_BLOCK:(col - COL_BX + 1) * COL_BLOCK] = acc
            ub_ref[:, c0:c0 + COL_BLOCK] = acc.astype(BF16)
            continue
        for hh in range(HEADS_PER_BLOCK):
            l0 = hh * LANES
            seg = acc[:, l0:l0 + LANES]
            if col == COL_AQ:
                seg = rope_a(seg) * (DA_DH ** -0.5 * LOG2E)
            elif col == COL_AK:
                seg = rope_a(seg)
            elif col == COL_CQ:
                seg = seg * (HEAD_DIM ** -0.5 * LOG2E)
            elif col == COL_DQ:
                seg = rope_d(_rms(seg) * qkg_ref[0:1, :]) * (HEAD_DIM ** -0.5 * LOG2E)
            elif col == COL_DKV and hh < GQA_KV_HEADS:
                seg = rope_d(_rms(seg) * qkg_ref[1:2, :])
            ub_ref[:, c0 + l0:c0 + l0 + LANES] = seg.astype(BF16)
            if col in (COL_AK, COL_AV):
                to_cache(ca_ref, col - COL_AK, l0, seg)
            elif col in (COL_CK, COL_CV):
                to_cache(cc_ref, col - COL_CK, l0, seg)
            elif col == COL_DKV:
                to_cache(cd_ref, hh // GQA_KV_HEADS, (hh % GQA_KV_HEADS) * LANES, seg)


def _inproj_call(x, mod, norm_g, w_in_b, qk_g, rope_tabs, caches, layer, *, latent):
    rows = x.shape[0]
    tm = IN_TM
    in_specs = [
        pl.BlockSpec((tm, D_MODEL), lambda i: (i, 0)),
        _mod_spec(layer, latent, tm, 1),
        pl.BlockSpec((None, 4, D_MODEL), lambda i: (layer, 0, 0)),
        pl.BlockSpec((None, D_MODEL, IN_WIDTH), lambda i: (layer, 0, 0), pipeline_mode=RESIDENT),
        pl.BlockSpec((None, 2, HEAD_DIM), lambda i: (layer, 0, 0)),
    ]
    args = [x, mod, norm_g, w_in_b, qk_g]
    out_specs = [
        pl.BlockSpec((tm, IN_WIDTH), lambda i: (i, 0)),
        pl.BlockSpec((tm, 2 * COL_BLOCK), lambda i: (i, 0)),
    ]
    out_shape = [
        jax.ShapeDtypeStruct((rows, IN_WIDTH), BF16),
        jax.ShapeDtypeStruct((rows, 2 * COL_BLOCK), F32),
    ]
    aliases = {}
    nbt = 1
    if latent:
        per_batch = DEC_SEQ // tm
        tab_spec = pl.BlockSpec((3, tm, LANES), lambda i: (0, i % per_batch, 0))
        in_specs += [tab_spec, tab_spec]
        args += [rope_tabs[0], rope_tabs[1]]
    else:
        nbt = tm // SEQ
        for n, cache in enumerate(caches):
            out_specs.append(pl.BlockSpec((nbt, None, 2, SEQ, cache.shape[-2], HEAD_DIM),
                                          lambda i: (i, layer, 0, 0, 0, 0)))
            out_shape.append(jax.ShapeDtypeStruct(cache.shape, F32))
            in_specs.append(pl.BlockSpec(memory_space=pl.ANY))
            args.append(cache)
            aliases[len(args) - 1] = 2 + n
    return pl.pallas_call(
        functools.partial(_inproj_kernel, latent=latent, nbt=nbt),
        grid=(rows // tm,),
        in_specs=in_specs,
        out_specs=out_specs,
        out_shape=out_shape,
        input_output_aliases=aliases,
        scratch_shapes=[pltpu.VMEM((tm, D_MODEL), BF16)],
        compiler_params=_params(("arbitrary",)),
        name="inproj_lat" if latent else "inproj_ctx",
    )(*args)


def _rope_tables():
    t = jnp.arange(DEC_SEQ)
    row = (t // GRID_W).astype(F32)[:, None]
    col = (t % GRID_W).astype(F32)[:, None]
    lane = np.arange(LANES)

    def table(d):
        m = d // 2
        freqs = ROPE_THETA ** (-jnp.arange(0, m, 2, dtype=F32) / m)
        within = lane % m
        k = within % (m // 2)
        hi = (within >= m // 2)[None, :]
        use_col = ((lane % d) >= m)[None, :]
        ang = jnp.where(use_col, col, row) * freqs[k][None, :]
        cos, sin = jnp.cos(ang), jnp.sin(ang)
        return jnp.stack([cos, jnp.where(hi, 0.0, -sin), jnp.where(hi, sin, 0.0)])

    return table(DA_DH), table(HEAD_DIM)


def _with_ones(v):
    return jnp.concatenate([v, jnp.ones_like(v)], axis=-1)


def _softmax_av(q, kt, v_ones):
    s = _dot(q, kt)
    e = jnp.exp2(s - s.max(axis=-1, keepdims=True)).astype(BF16)
    acc = _dot(e, v_ones)
    return acc[:, :LANES] * (1.0 / acc[:, LANES:])


def _fill_kv(kt_scr, v_scr, k_ref, v_ref, kc_ref, vc_ref):
    n_new = k_ref.shape[0]
    p = kc_ref.shape[0]
    kt_scr[:, 0:p] = kc_ref[...].T.astype(BF16)
    kt_scr[:, p:p + n_new] = k_ref[...].astype(F32).T.astype(BF16)
    v_scr[0:p, 0:LANES] = vc_ref[...].astype(BF16)
    v_scr[p:p + n_new, 0:LANES] = v_ref[...]
    v_scr[:, LANES:] = jnp.ones((p + n_new, LANES), BF16)


ATT_TQ = 2048
ATT_SUB = 256
LAT_KEYS = PAST_LEN + DEC_SEQ


def _diff_lambda(lamw_ref, lam_init):
    lw = lamw_ref[...]
    return (jnp.exp(jnp.sum(lw[0:1] * lw[1:2], axis=-1, keepdims=True))
            - jnp.exp(jnp.sum(lw[2:3] * lw[3:4], axis=-1, keepdims=True)) + lam_init)


def _diff_head(q, kt, v_ones, lam, lam_init):
    r = q.shape[0]
    lane = lax.broadcasted_iota(jnp.int32, q.shape, 1)
    zero = jnp.zeros_like(q)
    qs = jnp.concatenate([jnp.where(lane < DA_DH, q, zero), jnp.where(lane >= DA_DH, q, zero)], axis=0)
    o = _softmax_av(qs, kt, v_ones)
    return _rms(o[:r] - lam * o[r:]) * (1.0 - lam_init)


def _diff_kernel(lamw_ref, q_ref, k_ref, v_ref, kc_ref, vc_ref, o_ref, kt_scr, v_scr, *, lam_init):
    @pl.when(pl.program_id(2) == 0)
    def _():
        _fill_kv(kt_scr, v_scr, k_ref, v_ref, kc_ref, vc_ref)

    lam = _diff_lambda(lamw_ref, lam_init)
    for c in range(ATT_TQ // ATT_SUB):
        rows = slice(c * ATT_SUB, (c + 1) * ATT_SUB)
        o_ref[rows, :] = _diff_head(q_ref[rows, :], kt_scr[...], v_scr[...], lam, lam_init).astype(BF16)


def _cache_specs(layer):
    return [pl.BlockSpec((None, None, None, PAST_LEN, HEAD_DIM),
                         lambda b, h, i, kv=kv: (b, layer, kv, 0, h)) for kv in (0, 1)]


def _kv_scratch():
    return [pltpu.VMEM((HEAD_DIM, LAT_KEYS), BF16), pltpu.VMEM((LAT_KEYS, 2 * LANES), BF16)]


def _diff_call(ub, lam_w, cache, layer):
    tq = ATT_TQ
    nq = DEC_SEQ // tq
    lam_init = 0.8 - 0.6 * math.exp(-0.3 * layer)
    return pl.pallas_call(
        functools.partial(_diff_kernel, lam_init=lam_init),
        grid=(DEC_BATCH, DA_HEADS, nq),
        in_specs=[
            pl.BlockSpec((None, 4, DA_DH), lambda b, h, i: (layer, 0, 0)),
            pl.BlockSpec((tq, LANES), lambda b, h, i: (b * nq + i, COL_AQ * 4 + h)),
            pl.BlockSpec((DEC_SEQ, LANES), lambda b, h, i: (b, COL_AK * 4 + h)),
            pl.BlockSpec((DEC_SEQ, LANES), lambda b, h, i: (b, COL_AV * 4 + h)),
            *_cache_specs(layer),
        ],
        out_specs=pl.BlockSpec((tq, LANES), lambda b, h, i: (b * nq + i, h)),
        out_shape=jax.ShapeDtypeStruct((DEC_BATCH * DEC_SEQ, DA_HEADS * HEAD_DIM), BF16),
        scratch_shapes=_kv_scratch(),
        compiler_params=_params(("arbitrary", "arbitrary", "arbitrary")),
        name="diff_lat",
    )(lam_w, ub, ub, ub, cache, cache)


GQA_GROUP = GQA_Q_HEADS // GQA_KV_HEADS


def _gqa_kernel(q_ref, k_ref, v_ref, kc_ref, vc_ref, o_ref, kt_scr, v_scr):
    @pl.when(pl.program_id(2) == 0)
    def _():
        _fill_kv(kt_scr, v_scr, k_ref, v_ref, kc_ref, vc_ref)

    for c in range(ATT_TQ // ATT_SUB):
        r0 = c * ATT_SUB
        qs = jnp.concatenate([q_ref[r0:r0 + ATT_SUB, g * LANES:(g + 1) * LANES] for g in range(GQA_GROUP)], axis=0)
        o = _softmax_av(qs, kt_scr[...], v_scr[...])
        for g in range(GQA_GROUP):
            o_ref[r0:r0 + ATT_SUB, g * LANES:(g + 1) * LANES] = o[g * ATT_SUB:(g + 1) * ATT_SUB].astype(BF16)


def _gqa_call(ub, cache, layer):
    tq = ATT_TQ
    nq = DEC_SEQ // tq
    qw = GQA_GROUP * LANES
    return pl.pallas_call(
        _gqa_kernel,
        grid=(DEC_BATCH, GQA_KV_HEADS, nq),
        in_specs=[
            pl.BlockSpec((tq, qw), lambda b, n, i: (b * nq + i, COL_DQ * 4 // GQA_GROUP + n)),
            pl.BlockSpec((DEC_SEQ, LANES), lambda b, n, i: (b, COL_DKV * 4 + n)),
            pl.BlockSpec((DEC_SEQ, LANES), lambda b, n, i: (b, COL_DKV * 4 + GQA_KV_HEADS + n)),
            *_cache_specs(layer),
        ],
        out_specs=pl.BlockSpec((tq, qw), lambda b, n, i: (b * nq + i, n)),
        out_shape=jax.ShapeDtypeStruct((DEC_BATCH * DEC_SEQ, GQA_Q_HEADS * HEAD_DIM), BF16),
        scratch_shapes=_kv_scratch(),
        compiler_params=_params(("arbitrary", "arbitrary", "arbitrary")),
        name="attn_d_lat",
    )(ub, ub, ub, cache, cache)


def _attn_ctx_kernel(lamw_ref, u_ref, oa_ref, oc_ref, od_ref, *, lam_init):
    def head(col, hh):
        c0 = col * COL_BLOCK + hh * LANES
        return u_ref[:, c0:c0 + LANES]

    def keys_t(col, hh):
        return head(col, hh).astype(F32).T.astype(BF16)

    lam = _diff_lambda(lamw_ref, lam_init)
    for hh in range(DA_HEADS):
        o = _diff_head(head(COL_AQ, hh), keys_t(COL_AK, hh), _with_ones(head(COL_AV, hh)), lam, lam_init)
        oa_ref[:, hh * LANES:(hh + 1) * LANES] = o.astype(BF16)
    for hh in range(NA_HEADS):
        o = _softmax_av(head(COL_CQ, hh), keys_t(COL_CK, hh), _with_ones(head(COL_CV, hh)))
        oc_ref[:, hh * LANES:(hh + 1) * LANES] = o.astype(BF16)
    for n in range(GQA_KV_HEADS):
        qs = jnp.concatenate([head(COL_DQ, n * GQA_GROUP + g) for g in range(GQA_GROUP)], axis=0)
        o = _softmax_av(qs, keys_t(COL_DKV, n), _with_ones(head(COL_DKV, GQA_KV_HEADS + n)))
        for g in range(GQA_GROUP):
            hq = n * GQA_GROUP + g
            od_ref[:, hq * LANES:(hq + 1) * LANES] = o[g * SEQ:(g + 1) * SEQ].astype(BF16)


def _attn_ctx_call(ub, lam_w, layer):
    lam_init = 0.8 - 0.6 * math.exp(-0.3 * layer)
    out_spec = pl.BlockSpec((SEQ, COL_BLOCK), lambda b: (b, 0))
    out_sds = jax.ShapeDtypeStruct((BATCH * SEQ, COL_BLOCK), BF16)
    return pl.pallas_call(
        functools.partial(_attn_ctx_kernel, lam_init=lam_init),
        grid=(BATCH,),
        in_specs=[
            pl.BlockSpec((None, 4, DA_DH), lambda b: (layer, 0, 0)),
            pl.BlockSpec((SEQ, IN_WIDTH), lambda b: (b, 0)),
        ],
        out_specs=[out_spec, out_spec, out_spec],
        out_shape=[out_sds, out_sds, out_sds],
        compiler_params=_params(("arbitrary",)),
        name="attn_ctx",
    )(lam_w, ub)


def _na_win_row0(r0):
    return np.clip(r0 - NA_ROWS // 2, 0, GRID_H - NA_WIN_ROWS)


def _na_bias_tables(rpb):
    nd = 2 * NA_COLS - 1
    lo = (GRID_W - 1) - (NA_COLS - 1)
    period = 2 * GRID_W
    p = jnp.pad(rpb, ((0, 0), (0, 0), (0, 0), (lo, period - nd - lo)))
    tiled = jnp.tile(p, (1, 1, 1, GRID_W))[..., :GRID_W * (period - 1)]
    toe = tiled.reshape(DEPTH, NA_HEADS, 2 * NA_ROWS - 1, GRID_W, period - 1)[..., GRID_W - 1:]
    qc = np.arange(GRID_W)
    cstart = np.clip(qc - NA_COLS // 2, 0, GRID_W - NA_COLS)
    col_ok = (qc[None, :] >= cstart[:, None]) & (qc[None, :] < cstart[:, None] + NA_COLS)
    toe = jnp.where(col_ok, toe, NEG_BIG)
    masked = jnp.full((DEPTH, NA_HEADS, GRID_W, GRID_W), NEG_BIG, F32)
    cases = []
    for r0 in (0, NA_QROWS, GRID_H - NA_QROWS):
        w0 = int(_na_win_row0(r0))
        rows = []
        for qa in range(NA_QROWS):
            qr = r0 + qa
            rstart = int(np.clip(qr - NA_ROWS // 2, 0, GRID_H - NA_ROWS))
            blocks = []
            for kj in range(NA_WIN_ROWS):
                kr = w0 + kj
                inside = rstart <= kr < rstart + NA_ROWS
                blocks.append(toe[:, :, kr - qr + NA_ROWS - 1] if inside else masked)
            rows.append(jnp.concatenate(blocks, axis=-1))
        cases.append(jnp.concatenate(rows, axis=-2))
    return jnp.stack(cases, axis=2)


NA_CHAINS = 8
NA_BLOCKS = GRID_H // NA_QROWS


def _na_kernel(q_ref, k_ref, v_ref, kc_ref, vc_ref, bias_ref, o_ref, kt_scr, v_scr):
    step = pl.program_id(2)

    @pl.when(step == 0)
    def _():
        _fill_kv(kt_scr, v_scr, k_ref, v_ref, kc_ref, vc_ref)

    for c in range(NA_CHAINS):
        rb = step * NA_CHAINS + c
        rows = slice(c * NA_TQ, (c + 1) * NA_TQ)
        w0 = pl.multiple_of(
            PAST_LEN + jnp.clip(rb * NA_QROWS - NA_ROWS // 2, 0, GRID_H - NA_WIN_ROWS) * GRID_W,
            NA_QROWS * GRID_W)
        case = jnp.where(rb == 0, 0, jnp.where(rb == NA_BLOCKS - 1, 2, 1))
        q = q_ref[rows, :]
        s_loc = _dot(q, kt_scr[:, pl.ds(w0, NA_WIN)]) + bias_ref[case] * LOG2E
        s_ctx = _dot(q, kt_scr[:, 0:PAST_LEN])
        m = jnp.maximum(s_loc.max(axis=-1, keepdims=True), s_ctx.max(axis=-1, keepdims=True))
        acc = (_dot(jnp.exp2(s_loc - m).astype(BF16), v_scr[pl.ds(w0, NA_WIN), :])
               + _dot(jnp.exp2(s_ctx - m).astype(BF16), v_scr[0:PAST_LEN, :]))
        o_ref[rows, :] = (acc[:, :LANES] * (1.0 / acc[:, LANES:])).astype(BF16)


def _na_call(ub, cache, bias_tabs, layer):
    nsteps = NA_BLOCKS // NA_CHAINS
    tq = NA_CHAINS * NA_TQ
    return pl.pallas_call(
        _na_kernel,
        grid=(DEC_BATCH, NA_HEADS, nsteps),
        in_specs=[
            pl.BlockSpec((tq, LANES), lambda b, h, r: (b * nsteps + r, COL_CQ * 4 + h)),
            pl.BlockSpec((DEC_SEQ, LANES), lambda b, h, r: (b, COL_CK * 4 + h)),
            pl.BlockSpec((DEC_SEQ, LANES), lambda b, h, r: (b, COL_CV * 4 + h)),
            *_cache_specs(layer),
            pl.BlockSpec((None, None, 3, NA_TQ, NA_WIN), lambda b, h, r: (layer, h, 0, 0, 0)),
        ],
        out_specs=pl.BlockSpec((tq, LANES), lambda b, h, r: (b * nsteps + r, h)),
        out_shape=jax.ShapeDtypeStruct((DEC_BATCH * DEC_SEQ, NA_HEADS * HEAD_DIM), BF16),
        scratch_shapes=_kv_scratch(),
        compiler_params=_params(("arbitrary", "arbitrary", "arbitrary")),
        name="na_lat",
    )(ub, ub, ub, cache, cache, bias_tabs)


LRU_TC = 256
SUB = 8
PAD = 8


def _gelu_tanh(x):
    return x * (0.5 * (1.0 + jnp.tanh(math.sqrt(2.0 / math.pi) * (x + 0.044715 * (x * x * x)))))


def _lru_kernel(x_ref, g_ref, cw_ref, cb_ref, gw_ref, gb_ref, lam_ref, h0_ref,
                y_ref, st_ref, xp_scr, xc_scr, hf_scr, a_scr, b_scr, *, seq):
    n_chunks = seq // LRU_TC
    zeros_pad = jnp.zeros((PAD, LRU_WIDTH), F32)
    xp_scr[0:PAD, :] = zeros_pad
    xp_scr[PAD + seq:2 * PAD + seq, :] = zeros_pad
    xp_scr[PAD:PAD + seq, :] = x_ref[...]

    rowid = lax.broadcasted_iota(jnp.int32, (SUB, LRU_WIDTH), 0)

    def coeffs(xc, d):
        xcb = xc.astype(BF16)
        pre = []
        for gi in range(2):
            parts = [_dot(xcb[:, n * LRU_BS:(n + 1) * LRU_BS], gw_ref[d, gi, n].astype(BF16))
                     for n in range(LRU_BLOCKS)]
            pre.append(jnp.concatenate(parts, axis=-1) + gb_ref[d, gi:gi + 1, :])
        r = jax.nn.sigmoid(pre[0])
        i = jax.nn.sigmoid(pre[1])
        log_a = -LRU_C * r * jax.nn.softplus(-lam_ref[d:d + 1, :])
        a = jnp.exp(log_a)
        mult = jnp.sqrt(-jnp.tanh(log_a) * (a * a + 1.0))
        return a, mult * i * xc

    def scan_tile(a, b, carry, reverse):
        for d in (1, 2, 4):
            if reverse:
                keep = rowid < SUB - d
                sh = SUB - d
            else:
                keep = rowid >= d
                sh = d
            a_s = jnp.where(keep, pltpu.roll(a, sh, 0), 1.0)
            b_s = jnp.where(keep, pltpu.roll(b, sh, 0), 0.0)
            b = a * b_s + b
            a = a * a_s
        return a * carry + b

    carry = h0_ref[0:1, :]
    for c in range(n_chunks):
        r0 = c * LRU_TC
        xc = cb_ref[...] + sum(
            xp_scr[PAD + r0 + j - CONV_LEFT:PAD + r0 + j - CONV_LEFT + LRU_TC, :] * cw_ref[j:j + 1, :]
            for j in range(CONV_W))
        xc_scr[r0:r0 + LRU_TC, :] = xc
        a, b = coeffs(xc, 0)
        a_scr[...] = a
        b_scr[...] = b

        def body(t, carry, r0=r0):
            r = pl.multiple_of(t * SUB, SUB)
            h = scan_tile(a_scr[pl.ds(r, SUB), :], b_scr[pl.ds(r, SUB), :], carry, False)
            hf_scr[pl.ds(r0 + r, SUB), :] = h
            return h[SUB - 1:SUB, :]

        carry = lax.fori_loop(0, LRU_TC // SUB, body, carry)
    st_ref[0:1, :] = carry

    carry = h0_ref[1:2, :]
    for c in reversed(range(n_chunks)):
        r0 = c * LRU_TC
        a, b = coeffs(xc_scr[r0:r0 + LRU_TC, :], 1)
        a_scr[...] = a
        b_scr[...] = b

        def body(tt, carry):
            r = pl.multiple_of((LRU_TC // SUB - 1 - tt) * SUB, SUB)
            h = scan_tile(a_scr[pl.ds(r, SUB), :], b_scr[pl.ds(r, SUB), :], carry, True)
            b_scr[pl.ds(r, SUB), :] = h
            return h[0:1, :]

        carry = lax.fori_loop(0, LRU_TC // SUB, body, carry)
        y = (hf_scr[r0:r0 + LRU_TC, :] + b_scr[...]) * _gelu_tanh(g_ref[r0:r0 + LRU_TC, :])
        y_ref[r0:r0 + LRU_TC, :] = y.astype(BF16)
    st_ref[1:2, :] = carry


def _lru_call(uf, conv_w, conv_b, gate_w, gate_b, lam, h0, layer, *, latent):
    seq = DEC_SEQ if latent else SEQ
    nb = DEC_BATCH if latent else BATCH
    if latent:
        h0_spec = pl.BlockSpec((None, None, 2, LRU_WIDTH), lambda b: (b, layer, 0, 0))
    else:
        h0_spec = pl.BlockSpec((None, 2, LRU_WIDTH), lambda b: (b, 0, 0))
    return pl.pallas_call(
        functools.partial(_lru_kernel, seq=seq),
        grid=(nb,),
        in_specs=[
            pl.BlockSpec((seq, LRU_WIDTH), lambda b: (b, 0)),
            pl.BlockSpec((seq, LRU_WIDTH), lambda b: (b, 1)),
            pl.BlockSpec((None, CONV_W, LRU_WIDTH), lambda b: (layer, 0, 0)),
            pl.BlockSpec((None, 1, LRU_WIDTH), lambda b: (layer, 0, 0)),
            pl.BlockSpec((None, 2, 2, LRU_BLOCKS, LRU_BS, LRU_BS), lambda b: (layer, 0, 0, 0, 0, 0)),
            pl.BlockSpec((None, 2, 2, LRU_WIDTH), lambda b: (layer, 0, 0, 0)),
            pl.BlockSpec((None, 2, LRU_WIDTH), lambda b: (layer, 0, 0)),
            h0_spec,
        ],
        out_specs=[
            pl.BlockSpec((seq, LRU_WIDTH), lambda b: (b, 0)),
            pl.BlockSpec((None, 2, LRU_WIDTH), lambda b: (b, 0, 0)),
        ],
        out_shape=[
            jax.ShapeDtypeStruct((nb * seq, LRU_WIDTH), BF16),
            jax.ShapeDtypeStruct((nb, 2, LRU_WIDTH), F32),
        ],
        scratch_shapes=[
            pltpu.VMEM((seq + 2 * PAD, LRU_WIDTH), F32),
            pltpu.VMEM((seq, LRU_WIDTH), F32),
            pltpu.VMEM((seq, LRU_WIDTH), F32),
            pltpu.VMEM((LRU_TC, LRU_WIDTH), F32),
            pltpu.VMEM((LRU_TC, LRU_WIDTH), F32),
        ],
        compiler_params=_params(("arbitrary",)),
        name="lru_lat" if latent else "lru_ctx",
    )(uf, uf, conv_w, conv_b.reshape(DEPTH, 1, LRU_WIDTH), gate_w, gate_b, lam, h0)


OUT_TM = 512
OUT_SUB = 256


def _outproj_kernel(x_ref, ma_ref, mb_ref, mc_ref, md_ref, mod_ref, g_ref, w_ref, x1_ref, h2_ref):
    for c in range(OUT_TM // OUT_SUB):
        rows = slice(c * OUT_SUB, (c + 1) * OUT_SUB)
        y = None
        for n, m_ref in enumerate((ma_ref, mb_ref, mc_ref, md_ref)):
            part = _dot(m_ref[rows, :], w_ref[n * COL_BLOCK:(n + 1) * COL_BLOCK, :])
            y = part if y is None else y + part
        x1 = x_ref[rows, :] + mod_ref[2:3, :] * (_rms(y) * g_ref[1:2, :])
        x1_ref[rows, :] = x1
        h2 = _rms(x1) * g_ref[2:3, :]
        h2_ref[rows, :] = (h2 * (1.0 + mod_ref[4:5, :]) + mod_ref[3:4, :]).astype(BF16)


def _outproj_call(x, parts, mod, norm_g, w_out_b, layer, *, latent):
    rows = x.shape[0]
    tm = OUT_TM
    part_spec = pl.BlockSpec((tm, COL_BLOCK), lambda i: (i, 0))
    return pl.pallas_call(
        _outproj_kernel,
        grid=(rows // tm,),
        in_specs=[
            pl.BlockSpec((tm, D_MODEL), lambda i: (i, 0)),
            part_spec, part_spec, part_spec, part_spec,
            _mod_spec(layer, latent, tm, 1),
            pl.BlockSpec((None, 4, D_MODEL), lambda i: (layer, 0, 0)),
            pl.BlockSpec((None, MIX_WIDTH, D_MODEL), lambda i: (layer, 0, 0), pipeline_mode=RESIDENT),
        ],
        out_specs=[
            pl.BlockSpec((tm, D_MODEL), lambda i: (i, 0)),
            pl.BlockSpec((tm, D_MODEL), lambda i: (i, 0)),
        ],
        out_shape=[
            jax.ShapeDtypeStruct((rows, D_MODEL), F32),
            jax.ShapeDtypeStruct((rows, D_MODEL), BF16),
        ],
        compiler_params=_params(("arbitrary",)),
        name="outproj_lat" if latent else "outproj_ctx",
    )(x, *parts, mod, norm_g, w_out_b)


FFN_TM = 512
FFN_OUT_CHUNK = 512
FFN_TH = 512
FFN_SUB = 256


def _ffn_kernel(h_ref, x1_ref, mod_ref, g_ref, wa_ref, wg_ref, wo_ref, o_ref):
    j = pl.program_id(1)

    @pl.when(j == 0)
    def _():
        o_ref[...] = jnp.zeros_like(o_ref)

    h = h_ref[...]
    for s in range(FFN_TH // FFN_SUB):
        sl = slice(s * FFN_SUB, (s + 1) * FFN_SUB)
        a = _dot(h, wa_ref[:, sl])
        g = _dot(h, wg_ref[:, sl])
        t = ((g * jax.nn.sigmoid(g)) * a).astype(BF16)
        for n in range(D_MODEL // FFN_OUT_CHUNK):
            cols = slice(n * FFN_OUT_CHUNK, (n + 1) * FFN_OUT_CHUNK)
            o_ref[:, cols] += _dot(t, wo_ref[sl, cols])

    @pl.when(j == pl.num_programs(1) - 1)
    def _():
        o_ref[...] = x1_ref[...] + mod_ref[5:6, :] * (_rms(o_ref[...]) * g_ref[3:4, :])


def _ffn_call(h2, x1, mod, norm_g, w_ffn_in_b, w_ffn_out_b, layer, *, latent):
    rows = x1.shape[0]
    tm, th = FFN_TM, FFN_TH
    nh = FFN_HIDDEN // th
    return pl.pallas_call(
        _ffn_kernel,
        grid=(rows // tm, nh),
        in_specs=[
            pl.BlockSpec((tm, D_MODEL), lambda i, j: (i, 0)),
            pl.BlockSpec((tm, D_MODEL), lambda i, j: (i, 0)),
            _mod_spec(layer, latent, tm, 2),
            pl.BlockSpec((None, 4, D_MODEL), lambda i, j: (layer, 0, 0)),
            pl.BlockSpec((None, D_MODEL, th), lambda i, j: (layer, 0, j)),
            pl.BlockSpec((None, D_MODEL, th), lambda i, j: (layer, 0, nh + j)),
            pl.BlockSpec((None, th, D_MODEL), lambda i, j: (layer, j, 0)),
        ],
        out_specs=pl.BlockSpec((tm, D_MODEL), lambda i, j: (i, 0)),
        out_shape=jax.ShapeDtypeStruct((rows, D_MODEL), F32),
        compiler_params=_params(("arbitrary", "arbitrary")),
        name="ffn_lat" if latent else "ffn_ctx",
    )(h2, x1, mod, norm_g, w_ffn_in_b, w_ffn_in_b, w_ffn_out_b)


def kernel(x_prompt, x_sample, cache_diff_kv, state_rglru, cache_na_kv, cache_gqa_kv, c, c_ctx, w_mod, b_mod,
           norm_g, w_in, diff_lambda_w, lru_conv_w, lru_conv_b, lru_gate_w, lru_gate_b, lru_lambda, na_rpb,
           gqa_qk_g, w_out, w_ffn_in, w_ffn_out):
    cond = jnp.concatenate([c_ctx[None, :], c, jnp.zeros((MOD_ROWS - 1 - DEC_BATCH, D_MODEL), F32)], axis=0)
    mod = _mod_call(cond, w_mod, b_mod).reshape(DEPTH, MOD_ROWS, 6, D_MODEL)

    w_in_b = w_in.astype(BF16)
    w_out_b = w_out.astype(BF16)
    w_ffn_in_b = w_ffn_in.astype(BF16)
    w_ffn_out_b = w_ffn_out.astype(BF16)
    rope_tabs = _rope_tables()
    na_bias = _na_bias_tables(na_rpb)

    cache_a = cache_diff_kv.reshape(DEC_BATCH, DEPTH, 2, PAST_LEN, DA_HEADS * HEAD_DIM)
    cache_c = cache_na_kv.reshape(DEC_BATCH, DEPTH, 2, PAST_LEN, NA_HEADS * HEAD_DIM)
    cache_d = cache_gqa_kv.reshape(DEC_BATCH, DEPTH, 2, PAST_LEN, GQA_KV_HEADS * HEAD_DIM)
    zero_state = jnp.zeros((BATCH, 2, LRU_WIDTH), F32)

    def layer(x, l, latent, new_caches):
        outs = _inproj_call(x, mod, norm_g, w_in_b, gqa_qk_g, rope_tabs, new_caches, l, latent=latent)
        ub, uf = outs[0], outs[1]
        m_b, st = _lru_call(uf, lru_conv_w, lru_conv_b, lru_gate_w, lru_gate_b, lru_lambda,
                            state_rglru if latent else zero_state, l, latent=latent)
        if latent:
            m_a = _diff_call(ub, diff_lambda_w, cache_a, l)
            m_c = _na_call(ub, cache_c, na_bias, l)
            m_d = _gqa_call(ub, cache_d, l)
        else:
            m_a, m_c, m_d = _attn_ctx_call(ub, diff_lambda_w, l)
        x1, h2 = _outproj_call(x, (m_a, m_b, m_c, m_d), mod, norm_g, w_out_b, l, latent=latent)
        x_new = _ffn_call(h2, x1, mod, norm_g, w_ffn_in_b, w_ffn_out_b, l, latent=latent)
        return x_new, tuple(outs[2:]), st

    y = x_prompt.reshape(BATCH * SEQ, D_MODEL)
    new_caches = tuple(jnp.zeros((BATCH, DEPTH, 2, SEQ, heads, HEAD_DIM), F32)
                       for heads in (DA_HEADS, NA_HEADS, GQA_KV_HEADS))
    states = []
    for l in range(DEPTH):
        y, new_caches, st = layer(y, l, False, new_caches)
        states.append(st)
    y_prompt = y.reshape(BATCH, SEQ, D_MODEL)
    new_cache_diff_kv, new_cache_na_kv, new_cache_gqa_kv = new_caches
    new_state_rglru = jnp.stack(states, axis=1)

    z = x_sample.reshape(DEC_BATCH * DEC_SEQ, D_MODEL)
    for l in range(DEPTH):
        z, _, _ = layer(z, l, True, None)
    y_sample = z.reshape(DEC_BATCH, DEC_SEQ, D_MODEL)

    return (y_prompt, y_sample, new_cache_diff_kv, new_state_rglru, new_cache_na_kv, new_cache_gqa_kv)
```

```python
import functools
import math

import jax
import jax.numpy as jnp
import numpy as np
from jax import lax
from jax.experimental import pallas as pl
from jax.experimental.pallas import tpu as pltpu

D_MODEL = 2048
BATCH = 16
SEQ = 256
DEPTH = 4
DEC_BATCH = 4
DEC_SEQ = 2048
PAST_LEN = 512
GRID_W = 64
GRID_H = DEC_SEQ // GRID_W
HEAD_DIM = 128
EPS = 1e-6
ROPE_THETA = 10000.0
DA_HEADS = 4
DA_DH = HEAD_DIM // 2
LRU_WIDTH = 512
LRU_BLOCKS = 4
LRU_BS = LRU_WIDTH // LRU_BLOCKS
LRU_C = 8.0
CONV_W = 4
CONV_LEFT = 2
NA_HEADS = 4
NA_ROWS = 8
NA_COLS = 16
GQA_Q_HEADS = 4
GQA_KV_HEADS = 2
IN_WIDTH = 5120
MIX_WIDTH = 2048
FFN_HIDDEN = 5632
N_MOD = 6 * D_MODEL

F32 = jnp.float32
BF16 = jnp.bfloat16

VMEM_LIMIT_BYTES = 56 * 1024 * 1024
LANES = 128
LOG2E = math.log2(math.e)

COL_AQ, COL_AK, COL_AV, COL_BX, COL_BG, COL_CQ, COL_CK, COL_CV, COL_DQ, COL_DKV = range(10)
N_COL_BLOCKS = 10
COL_BLOCK = 512
HEADS_PER_BLOCK = COL_BLOCK // LANES

NA_QROWS = 4
NA_WIN_ROWS = NA_QROWS + NA_ROWS
NA_TQ = NA_QROWS * GRID_W
NA_WIN = NA_WIN_ROWS * GRID_W
NEG_BIG = -1e30

RESIDENT = pl.Buffered(1)


def _params(sem):
    return pltpu.CompilerParams(dimension_semantics=sem, vmem_limit_bytes=VMEM_LIMIT_BYTES)


def _rms(x):
    return x * lax.rsqrt(jnp.mean(x * x, axis=-1, keepdims=True) + EPS)


def _dot(a, b):
    return jnp.dot(a, b, preferred_element_type=F32)


def _dot_nt(a, b):
    return lax.dot_general(a, b, (((1,), (1,)), ((), ())), preferred_element_type=F32)


MOD_ROWS = 8
MOD_TN = 1024


def _mod_kernel(cond_ref, w_ref, b_ref, o_ref):
    c = cond_ref[...]
    s = (c * jax.nn.sigmoid(c)).astype(BF16)
    o_ref[...] = _dot(s, w_ref[...].astype(BF16)) + b_ref[...]


def _mod_call(cond, w_mod, b_mod):
    return pl.pallas_call(
        _mod_kernel,
        grid=(DEPTH, N_MOD // MOD_TN),
        in_specs=[
            pl.BlockSpec((MOD_ROWS, D_MODEL), lambda l, n: (0, 0)),
            pl.BlockSpec((None, D_MODEL, MOD_TN), lambda l, n: (l, 0, n)),
            pl.BlockSpec((None, 1, MOD_TN), lambda l, n: (l, 0, n)),
        ],
        out_specs=pl.BlockSpec((None, MOD_ROWS, MOD_TN), lambda l, n: (l, 0, n)),
        out_shape=jax.ShapeDtypeStruct((DEPTH, MOD_ROWS, N_MOD), F32),
        compiler_params=_params(("arbitrary", "arbitrary")),
        name="mod",
    )(cond, w_mod, b_mod.reshape(DEPTH, 1, N_MOD))


def _mod_spec(layer, latent, tm, ngrid):
    if latent:
        per_batch = DEC_SEQ // tm
        idx = lambda i: 1 + i // per_batch
    else:
        idx = lambda i: 0
    if ngrid == 1:
        return pl.BlockSpec((None, None, 6, D_MODEL), lambda i: (layer, idx(i), 0, 0))
    return pl.BlockSpec((None, None, 6, D_MODEL), lambda i, j: (layer, idx(i), 0, 0))


IN_TM = 256


def _rope(seg, cos, sin_lo, sin_hi, off):
    return (seg * cos + pltpu.roll(seg, LANES - off, 1) * sin_lo + pltpu.roll(seg, off, 1) * sin_hi)


def _inproj_kernel(*refs, latent, nbt):
    if latent:
        (x_ref, mod_ref, g_ref, w_ref, qkg_ref, ra_ref, rd_ref, ub_ref, uf_ref, h_scr) = refs
        ca_ref = cc_ref = cd_ref = None
    else:
        (x_ref, mod_ref, g_ref, w_ref, qkg_ref, _, _, _, ub_ref, uf_ref, ca_ref, cc_ref, cd_ref, h_scr) = refs

    gain = g_ref[0:1, :] * (1.0 + mod_ref[1:2, :])
    h_scr[...] = (_rms(x_ref[...]) * gain + mod_ref[0:1, :]).astype(BF16)

    def rope_a(seg):
        return _rope(seg, ra_ref[0], ra_ref[1], ra_ref[2], DA_DH // 4) if latent else seg

    def rope_d(seg):
        return _rope(seg, rd_ref[0], rd_ref[1], rd_ref[2], HEAD_DIM // 4) if latent else seg

    def to_cache(c_ref, kv, lane0, val):
        if latent:
            return
        for bb in range(nbt):
            c_ref[bb, kv, :, lane0 // LANES, :] = val[bb * SEQ:(bb + 1) * SEQ]

    for col in range(N_COL_BLOCKS):
        c0 = col * COL_BLOCK
        acc = _dot(h_scr[...], w_ref[:, c0:c0 + COL_BLOCK])
        if col in (COL_BX, COL_BG):
            uf_ref[:, (col - COL_BX) * COL_BLOCK:(col - COL_BX + 1) * COL_BLOCK] = acc
            ub_ref[:, c0:c0 + COL_BLOCK] = acc.astype(BF16)
            continue
        for hh in range(HEADS_PER_BLOCK):
            l0 = hh * LANES
            seg = acc[:, l0:l0 + LANES]
            if col == COL_AQ:
                seg = rope_a(seg) * (DA_DH ** -0.5 * LOG2E)
            elif col == COL_AK:
                seg = rope_a(seg)
            elif col == COL_CQ:
                seg = seg * (HEAD_DIM ** -0.5 * LOG2E)
            elif col == COL_DQ:
                seg = rope_d(_rms(seg) * qkg_ref[0:1, :]) * (HEAD_DIM ** -0.5 * LOG2E)
            elif col == COL_DKV and hh < GQA_KV_HEADS:
                seg = rope_d(_rms(seg) * qkg_ref[1:2, :])
            ub_ref[:, c0 + l0:c0 + l0 + LANES] = seg.astype(BF16)
            if col in (COL_AK, COL_AV):
                to_cache(ca_ref, col - COL_AK, l0, seg)
            elif col in (COL_CK, COL_CV):
                to_cache(cc_ref, col - COL_CK, l0, seg)
            elif col == COL_DKV:
                to_cache(cd_ref, hh // GQA_KV_HEADS, (hh % GQA_KV_HEADS) * LANES, seg)


def _inproj_call(x, mod, norm_g, w_in_b, qk_g, rope_tabs, caches, layer, *, latent):
    rows = x.shape[0]
    tm = IN_TM
    in_specs = [
        pl.BlockSpec((tm, D_MODEL), lambda i: (i, 0)),
        _mod_spec(layer, latent, tm, 1),
        pl.BlockSpec((None, 4, D_MODEL), lambda i: (layer, 0, 0)),
        pl.BlockSpec((None, D_MODEL, IN_WIDTH), lambda i: (layer, 0, 0), pipeline_mode=RESIDENT),
        pl.BlockSpec((None, 2, HEAD_DIM), lambda i: (layer, 0, 0)),
    ]
    args = [x, mod, norm_g, w_in_b, qk_g]
    out_specs = [
        pl.BlockSpec((tm, IN_WIDTH), lambda i: (i, 0)),
        pl.BlockSpec((tm, 2 * COL_BLOCK), lambda i: (i, 0)),
    ]
    out_shape = [
        jax.ShapeDtypeStruct((rows, IN_WIDTH), BF16),
        jax.ShapeDtypeStruct((rows, 2 * COL_BLOCK), F32),
    ]
    aliases = {}
    nbt = 1
    if latent:
        per_batch = DEC_SEQ // tm
        tab_spec = pl.BlockSpec((3, tm, LANES), lambda i: (0, i % per_batch, 0))
        in_specs += [tab_spec, tab_spec]
        args += [rope_tabs[0], rope_tabs[1]]
    else:
        nbt = tm // SEQ
        for n, cache in enumerate(caches):
            out_specs.append(pl.BlockSpec((nbt, None, 2, SEQ, cache.shape[-2], HEAD_DIM),
                                          lambda i: (i, layer, 0, 0, 0, 0)))
            out_shape.append(jax.ShapeDtypeStruct(cache.shape, F32))
            in_specs.append(pl.BlockSpec(memory_space=pl.ANY))
            args.append(cache)
            aliases[len(args) - 1] = 2 + n
    return pl.pallas_call(
        functools.partial(_inproj_kernel, latent=latent, nbt=nbt),
        grid=(rows // tm,),
        in_specs=in_specs,
        out_specs=out_specs,
        out_shape=out_shape,
        input_output_aliases=aliases,
        scratch_shapes=[pltpu.VMEM((tm, D_MODEL), BF16)],
        compiler_params=_params(("arbitrary",)),
        name="inproj_lat" if latent else "inproj_ctx",
    )(*args)


def _rope_tables():
    t = jnp.arange(DEC_SEQ)
    row = (t // GRID_W).astype(F32)[:, None]
    col = (t % GRID_W).astype(F32)[:, None]
    lane = np.arange(LANES)

    def table(d):
        m = d // 2
        freqs = ROPE_THETA ** (-jnp.arange(0, m, 2, dtype=F32) / m)
        within = lane % m
        k = within % (m // 2)
        hi = (within >= m // 2)[None, :]
        use_col = ((lane % d) >= m)[None, :]
        ang = jnp.where(use_col, col, row) * freqs[k][None, :]
        cos, sin = jnp.cos(ang), jnp.sin(ang)
        return jnp.stack([cos, jnp.where(hi, 0.0, -sin), jnp.where(hi, sin, 0.0)])

    return table(DA_DH), table(HEAD_DIM)


def _with_ones(v):
    return jnp.concatenate([v, jnp.ones_like(v)], axis=-1)


def _softmax_av(q, kt, v_ones):
    s = _dot(q, kt)
    e = jnp.exp2(s - s.max(axis=-1, keepdims=True)).astype(BF16)
    acc = _dot(e, v_ones)
    return acc[:, :LANES] * (1.0 / acc[:, LANES:])


def _fill_kv(kt_scr, v_scr, k_ref, v_ref, kc_ref, vc_ref):
    n_new = k_ref.shape[0]
    p = kc_ref.shape[0]
    kt_scr[:, 0:p] = kc_ref[...].T.astype(BF16)
    kt_scr[:, p:p + n_new] = k_ref[...].astype(F32).T.astype(BF16)
    v_scr[0:p, 0:LANES] = vc_ref[...].astype(BF16)
    v_scr[p:p + n_new, 0:LANES] = v_ref[...]
    v_scr[:, LANES:] = jnp.ones((p + n_new, LANES), BF16)


ATT_TQ = 2048
ATT_SUB = 256
LAT_KEYS = PAST_LEN + DEC_SEQ


def _diff_lambda(lamw_ref, lam_init):
    lw = lamw_ref[...]
    return (jnp.exp(jnp.sum(lw[0:1] * lw[1:2], axis=-1, keepdims=True))
            - jnp.exp(jnp.sum(lw[2:3] * lw[3:4], axis=-1, keepdims=True)) + lam_init)


def _diff_head(q, kt, v_ones, lam, lam_init):
    r = q.shape[0]
    lane = lax.broadcasted_iota(jnp.int32, q.shape, 1)
    zero = jnp.zeros_like(q)
    qs = jnp.concatenate([jnp.where(lane < DA_DH, q, zero), jnp.where(lane >= DA_DH, q, zero)], axis=0)
    o = _softmax_av(qs, kt, v_ones)
    return _rms(o[:r] - lam * o[r:]) * (1.0 - lam_init)


def _diff_kernel(lamw_ref, q_ref, k_ref, v_ref, kc_ref, vc_ref, o_ref, kt_scr, v_scr, *, lam_init):
    @pl.when(pl.program_id(2) == 0)
    def _():
        _fill_kv(kt_scr, v_scr, k_ref, v_ref, kc_ref, vc_ref)

    lam = _diff_lambda(lamw_ref, lam_init)
    for c in range(ATT_TQ // ATT_SUB):
        rows = slice(c * ATT_SUB, (c + 1) * ATT_SUB)
        o_ref[rows, :] = _diff_head(q_ref[rows, :], kt_scr[...], v_scr[...], lam, lam_init).astype(BF16)


def _cache_specs(layer):
    return [pl.BlockSpec((None, None, None, PAST_LEN, HEAD_DIM),
                         lambda b, h, i, kv=kv: (b, layer, kv, 0, h)) for kv in (0, 1)]


def _kv_scratch():
    return [pltpu.VMEM((HEAD_DIM, LAT_KEYS), BF16), pltpu.VMEM((LAT_KEYS, 2 * LANES), BF16)]


def _diff_call(ub, lam_w, cache, layer):
    tq = ATT_TQ
    nq = DEC_SEQ // tq
    lam_init = 0.8 - 0.6 * math.exp(-0.3 * layer)
    return pl.pallas_call(
        functools.partial(_diff_kernel, lam_init=lam_init),
        grid=(DEC_BATCH, DA_HEADS, nq),
        in_specs=[
            pl.BlockSpec((None, 4, DA_DH), lambda b, h, i: (layer, 0, 0)),
            pl.BlockSpec((tq, LANES), lambda b, h, i: (b * nq + i, COL_AQ * 4 + h)),
            pl.BlockSpec((DEC_SEQ, LANES), lambda b, h, i: (b, COL_AK * 4 + h)),
            pl.BlockSpec((DEC_SEQ, LANES), lambda b, h, i: (b, COL_AV * 4 + h)),
            *_cache_specs(layer),
        ],
        out_specs=pl.BlockSpec((tq, LANES), lambda b, h, i: (b * nq + i, h)),
        out_shape=jax.ShapeDtypeStruct((DEC_BATCH * DEC_SEQ, DA_HEADS * HEAD_DIM), BF16),
        scratch_shapes=_kv_scratch(),
        compiler_params=_params(("arbitrary", "arbitrary", "arbitrary")),
        name="diff_lat",
    )(lam_w, ub, ub, ub, cache, cache)


GQA_GROUP = GQA_Q_HEADS // GQA_KV_HEADS


def _gqa_kernel(q_ref, k_ref, v_ref, kc_ref, vc_ref, o_ref, kt_scr, v_scr):
    @pl.when(pl.program_id(2) == 0)
    def _():
        _fill_kv(kt_scr, v_scr, k_ref, v_ref, kc_ref, vc_ref)

    for c in range(ATT_TQ // ATT_SUB):
        r0 = c * ATT_SUB
        qs = jnp.concatenate([q_ref[r0:r0 + ATT_SUB, g * LANES:(g + 1) * LANES] for g in range(GQA_GROUP)], axis=0)
        o = _softmax_av(qs, kt_scr[...], v_scr[...])
        for g in range(GQA_GROUP):
            o_ref[r0:r0 + ATT_SUB, g * LANES:(g + 1) * LANES] = o[g * ATT_SUB:(g + 1) * ATT_SUB].astype(BF16)


def _gqa_call(ub, cache, layer):
    tq = ATT_TQ
    nq = DEC_SEQ // tq
    qw = GQA_GROUP * LANES
    return pl.pallas_call(
        _gqa_kernel,
        grid=(DEC_BATCH, GQA_KV_HEADS, nq),
        in_specs=[
            pl.BlockSpec((tq, qw), lambda b, n, i: (b * nq + i, COL_DQ * 4 // GQA_GROUP + n)),
            pl.BlockSpec((DEC_SEQ, LANES), lambda b, n, i: (b, COL_DKV * 4 + n)),
            pl.BlockSpec((DEC_SEQ, LANES), lambda b, n, i: (b, COL_DKV * 4 + GQA_KV_HEADS + n)),
            *_cache_specs(layer),
        ],
        out_specs=pl.BlockSpec((tq, qw), lambda b, n, i: (b * nq + i, n)),
        out_shape=jax.ShapeDtypeStruct((DEC_BATCH * DEC_SEQ, GQA_Q_HEADS * HEAD_DIM), BF16),
        scratch_shapes=_kv_scratch(),
        compiler_params=_params(("arbitrary", "arbitrary", "arbitrary")),
        name="attn_d_lat",
    )(ub, ub, ub, cache, cache)


def _attn_ctx_kernel(lamw_ref, u_ref, oa_ref, oc_ref, od_ref, *, lam_init):
    def head(col, hh):
        c0 = col * COL_BLOCK + hh * LANES
        return u_ref[:, c0:c0 + LANES]

    def keys_t(col, hh):
        return head(col, hh).astype(F32).T.astype(BF16)

    lam = _diff_lambda(lamw_ref, lam_init)
    for hh in range(DA_HEADS):
        o = _diff_head(head(COL_AQ, hh), keys_t(COL_AK, hh), _with_ones(head(COL_AV, hh)), lam, lam_init)
        oa_ref[:, hh * LANES:(hh + 1) * LANES] = o.astype(BF16)
    for hh in range(NA_HEADS):
        o = _softmax_av(head(COL_CQ, hh), keys_t(COL_CK, hh), _with_ones(head(COL_CV, hh)))
        oc_ref[:, hh * LANES:(hh + 1) * LANES] = o.astype(BF16)
    for n in range(GQA_KV_HEADS):
        qs = jnp.concatenate([head(COL_DQ, n * GQA_GROUP + g) for g in range(GQA_GROUP)], axis=0)
        o = _softmax_av(qs, keys_t(COL_DKV, n), _with_ones(head(COL_DKV, GQA_KV_HEADS + n)))
        for g in range(GQA_GROUP):
            hq = n * GQA_GROUP + g
            od_ref[:, hq * LANES:(hq + 1) * LANES] = o[g * SEQ:(g + 1) * SEQ].astype(BF16)


def _attn_ctx_call(ub, lam_w, layer):
    lam_init = 0.8 - 0.6 * math.exp(-0.3 * layer)
    out_spec = pl.BlockSpec((SEQ, COL_BLOCK), lambda b: (b, 0))
    out_sds = jax.ShapeDtypeStruct((BATCH * SEQ, COL_BLOCK), BF16)
    return pl.pallas_call(
        functools.partial(_attn_ctx_kernel, lam_init=lam_init),
        grid=(BATCH,),
        in_specs=[
            pl.BlockSpec((None, 4, DA_DH), lambda b: (layer, 0, 0)),
            pl.BlockSpec((SEQ, IN_WIDTH), lambda b: (b, 0)),
        ],
        out_specs=[out_spec, out_spec, out_spec],
        out_shape=[out_sds, out_sds, out_sds],
        compiler_params=_params(("arbitrary",)),
        name="attn_ctx",
    )(lam_w, ub)


def _na_win_row0(r0):
    return np.clip(r0 - NA_ROWS // 2, 0, GRID_H - NA_WIN_ROWS)


def _na_bias_tables(rpb):
    nd = 2 * NA_COLS - 1
    lo = (GRID_W - 1) - (NA_COLS - 1)
    period = 2 * GRID_W
    p = jnp.pad(rpb, ((0, 0), (0, 0), (0, 0), (lo, period - nd - lo)))
    tiled = jnp.tile(p, (1, 1, 1, GRID_W))[..., :GRID_W * (period - 1)]
    toe = tiled.reshape(DEPTH, NA_HEADS, 2 * NA_ROWS - 1, GRID_W, period - 1)[..., GRID_W - 1:]
    qc = np.arange(GRID_W)
    cstart = np.clip(qc - NA_COLS // 2, 0, GRID_W - NA_COLS)
    col_ok = (qc[None, :] >= cstart[:, None]) & (qc[None, :] < cstart[:, None] + NA_COLS)
    toe = jnp.where(col_ok, toe, NEG_BIG)
    masked = jnp.full((DEPTH, NA_HEADS, GRID_W, GRID_W), NEG_BIG, F32)
    cases = []
    for r0 in (0, NA_QROWS, GRID_H - NA_QROWS):
        w0 = int(_na_win_row0(r0))
        rows = []
        for qa in range(NA_QROWS):
            qr = r0 + qa
            rstart = int(np.clip(qr - NA_ROWS // 2, 0, GRID_H - NA_ROWS))
            blocks = []
            for kj in range(NA_WIN_ROWS):
                kr = w0 + kj
                inside = rstart <= kr < rstart + NA_ROWS
                blocks.append(toe[:, :, kr - qr + NA_ROWS - 1] if inside else masked)
            rows.append(jnp.concatenate(blocks, axis=-1))
        cases.append(jnp.concatenate(rows, axis=-2))
    return jnp.stack(cases, axis=2)


NA_CHAINS = 8
NA_BLOCKS = GRID_H // NA_QROWS


def _na_kernel(q_ref, k_ref, v_ref, kc_ref, vc_ref, bias_ref, o_ref, kt_scr, v_scr):
    step = pl.program_id(2)

    @pl.when(step == 0)
    def _():
        _fill_kv(kt_scr, v_scr, k_ref, v_ref, kc_ref, vc_ref)

    for c in range(NA_CHAINS):
        rb = step * NA_CHAINS + c
        rows = slice(c * NA_TQ, (c + 1) * NA_TQ)
        w0 = pl.multiple_of(
            PAST_LEN + jnp.clip(rb * NA_QROWS - NA_ROWS // 2, 0, GRID_H - NA_WIN_ROWS) * GRID_W,
            NA_QROWS * GRID_W)
        case = jnp.where(rb == 0, 0, jnp.where(rb == NA_BLOCKS - 1, 2, 1))
        q = q_ref[rows, :]
        s_loc = _dot(q, kt_scr[:, pl.ds(w0, NA_WIN)]) + bias_ref[case] * LOG2E
        s_ctx = _dot(q, kt_scr[:, 0:PAST_LEN])
        m = jnp.maximum(s_loc.max(axis=-1, keepdims=True), s_ctx.max(axis=-1, keepdims=True))
        acc = (_dot(jnp.exp2(s_loc - m).astype(BF16), v_scr[pl.ds(w0, NA_WIN), :])
               + _dot(jnp.exp2(s_ctx - m).astype(BF16), v_scr[0:PAST_LEN, :]))
        o_ref[rows, :] = (acc[:, :LANES] * (1.0 / acc[:, LANES:])).astype(BF16)


def _na_call(ub, cache, bias_tabs, layer):
    nsteps = NA_BLOCKS // NA_CHAINS
    tq = NA_CHAINS * NA_TQ
    return pl.pallas_call(
        _na_kernel,
        grid=(DEC_BATCH, NA_HEADS, nsteps),
        in_specs=[
            pl.BlockSpec((tq, LANES), lambda b, h, r: (b * nsteps + r, COL_CQ * 4 + h)),
            pl.BlockSpec((DEC_SEQ, LANES), lambda b, h, r: (b, COL_CK * 4 + h)),
            pl.BlockSpec((DEC_SEQ, LANES), lambda b, h, r: (b, COL_CV * 4 + h)),
            *_cache_specs(layer),
            pl.BlockSpec((None, None, 3, NA_TQ, NA_WIN), lambda b, h, r: (layer, h, 0, 0, 0)),
        ],
        out_specs=pl.BlockSpec((tq, LANES), lambda b, h, r: (b * nsteps + r, h)),
        out_shape=jax.ShapeDtypeStruct((DEC_BATCH * DEC_SEQ, NA_HEADS * HEAD_DIM), BF16),
        scratch_shapes=_kv_scratch(),
        compiler_params=_params(("arbitrary", "arbitrary", "arbitrary")),
        name="na_lat",
    )(ub, ub, ub, cache, cache, bias_tabs)


LRU_TC = 256
SUB = 8
PAD = 8


def _gelu_tanh(x):
    return x * (0.5 * (1.0 + jnp.tanh(math.sqrt(2.0 / math.pi) * (x + 0.044715 * (x * x * x)))))


def _lru_kernel(x_ref, g_ref, cw_ref, cb_ref, gw_ref, gb_ref, lam_ref, h0_ref,
                y_ref, st_ref, xp_scr, xc_scr, hf_scr, a_scr, b_scr, *, seq):
    n_chunks = seq // LRU_TC
    zeros_pad = jnp.zeros((PAD, LRU_WIDTH), F32)
    xp_scr[0:PAD, :] = zeros_pad
    xp_scr[PAD + seq:2 * PAD + seq, :] = zeros_pad
    xp_scr[PAD:PAD + seq, :] = x_ref[...]

    rowid = lax.broadcasted_iota(jnp.int32, (SUB, LRU_WIDTH), 0)

    def coeffs(xc, d):
        xcb = xc.astype(BF16)
        pre = []
        for gi in range(2):
            parts = [_dot(xcb[:, n * LRU_BS:(n + 1) * LRU_BS], gw_ref[d, gi, n].astype(BF16))
                     for n in range(LRU_BLOCKS)]
            pre.append(jnp.concatenate(parts, axis=-1) + gb_ref[d, gi:gi + 1, :])
        r = jax.nn.sigmoid(pre[0])
        i = jax.nn.sigmoid(pre[1])
        log_a = -LRU_C * r * jax.nn.softplus(-lam_ref[d:d + 1, :])
        a = jnp.exp(log_a)
        mult = jnp.sqrt(-jnp.tanh(log_a) * (a * a + 1.0))
        return a, mult * i * xc

    def scan_tile(a, b, carry, reverse):
        for d in (1, 2, 4):
            if reverse:
                keep = rowid < SUB - d
                sh = SUB - d
            else:
                keep = rowid >= d
                sh = d
            a_s = jnp.where(keep, pltpu.roll(a, sh, 0), 1.0)
            b_s = jnp.where(keep, pltpu.roll(b, sh, 0), 0.0)
            b = a * b_s + b
            a = a * a_s
        return a * carry + b

    carry = h0_ref[0:1, :]
    for c in range(n_chunks):
        r0 = c * LRU_TC
        xc = cb_ref[...] + sum(
            xp_scr[PAD + r0 + j - CONV_LEFT:PAD + r0 + j - CONV_LEFT + LRU_TC, :] * cw_ref[j:j + 1, :]
            for j in range(CONV_W))
        xc_scr[r0:r0 + LRU_TC, :] = xc
        a, b = coeffs(xc, 0)
        a_scr[...] = a
        b_scr[...] = b

        def body(t, carry, r0=r0):
            r = pl.multiple_of(t * SUB, SUB)
            h = scan_tile(a_scr[pl.ds(r, SUB), :], b_scr[pl.ds(r, SUB), :], carry, False)
            hf_scr[pl.ds(r0 + r, SUB), :] = h
            return h[SUB - 1:SUB, :]

        carry = lax.fori_loop(0, LRU_TC // SUB, body, carry)
    st_ref[0:1, :] = carry

    carry = h0_ref[1:2, :]
    for c in reversed(range(n_chunks)):
        r0 = c * LRU_TC
        a, b = coeffs(xc_scr[r0:r0 + LRU_TC, :], 1)
        a_scr[...] = a
        b_scr[...] = b

        def body(tt, carry):
            r = pl.multiple_of((LRU_TC // SUB - 1 - tt) * SUB, SUB)
            h = scan_tile(a_scr[pl.ds(r, SUB), :], b_scr[pl.ds(r, SUB), :], carry, True)
            b_scr[pl.ds(r, SUB), :] = h
            return h[0:1, :]

        carry = lax.fori_loop(0, LRU_TC // SUB, body, carry)
        y = (hf_scr[r0:r0 + LRU_TC, :] + b_scr[...]) * _gelu_tanh(g_ref[r0:r0 + LRU_TC, :])
        y_ref[r0:r0 + LRU_TC, :] = y.astype(BF16)
    st_ref[1:2, :] = carry


def _lru_call(uf, conv_w, conv_b, gate_w, gate_b, lam, h0, layer, *, latent):
    seq = DEC_SEQ if latent else SEQ
    nb = DEC_BATCH if latent else BATCH
    if latent:
        h0_spec = pl.BlockSpec((None, None, 2, LRU_WIDTH), lambda b: (b, layer, 0, 0))
    else:
        h0_spec = pl.BlockSpec((None, 2, LRU_WIDTH), lambda b: (b, 0, 0))
    return pl.pallas_call(
        functools.partial(_lru_kernel, seq=seq),
        grid=(nb,),
        in_specs=[
            pl.BlockSpec((seq, LRU_WIDTH), lambda b: (b, 0)),
            pl.BlockSpec((seq, LRU_WIDTH), lambda b: (b, 1)),
            pl.BlockSpec((None, CONV_W, LRU_WIDTH), lambda b: (layer, 0, 0)),
            pl.BlockSpec((None, 1, LRU_WIDTH), lambda b: (layer, 0, 0)),
            pl.BlockSpec((None, 2, 2, LRU_BLOCKS, LRU_BS, LRU_BS), lambda b: (layer, 0, 0, 0, 0, 0)),
            pl.BlockSpec((None, 2, 2, LRU_WIDTH), lambda b: (layer, 0, 0, 0)),
            pl.BlockSpec((None, 2, LRU_WIDTH), lambda b: (layer, 0, 0)),
            h0_spec,
        ],
        out_specs=[
            pl.BlockSpec((seq, LRU_WIDTH), lambda b: (b, 0)),
            pl.BlockSpec((None, 2, LRU_WIDTH), lambda b: (b, 0, 0)),
        ],
        out_shape=[
            jax.ShapeDtypeStruct((nb * seq, LRU_WIDTH), BF16),
            jax.ShapeDtypeStruct((nb, 2, LRU_WIDTH), F32),
        ],
        scratch_shapes=[
            pltpu.VMEM((seq + 2 * PAD, LRU_WIDTH), F32),
            pltpu.VMEM((seq, LRU_WIDTH), F32),
            pltpu.VMEM((seq, LRU_WIDTH), F32),
            pltpu.VMEM((LRU_TC, LRU_WIDTH), F32),
            pltpu.VMEM((LRU_TC, LRU_WIDTH), F32),
        ],
        compiler_params=_params(("arbitrary",)),
        name="lru_lat" if latent else "lru_ctx",
    )(uf, uf, conv_w, conv_b.reshape(DEPTH, 1, LRU_WIDTH), gate_w, gate_b, lam, h0)


OUT_TM = 512
OUT_SUB = 256


def _outproj_kernel(x_ref, ma_ref, mb_ref, mc_ref, md_ref, mod_ref, g_ref, w_ref, x1_ref, h2_ref):
    gate_gain = mod_ref[2:3, :] * g_ref[1:2, :]
    ffn_gain = g_ref[2:3, :] * (1.0 + mod_ref[4:5, :])
    for c in range(OUT_TM // OUT_SUB):
        rows = slice(c * OUT_SUB, (c + 1) * OUT_SUB)
        y = None
        for n, m_ref in enumerate((ma_ref, mb_ref, mc_ref, md_ref)):
            part = _dot(m_ref[rows, :], w_ref[n * COL_BLOCK:(n + 1) * COL_BLOCK, :])
            y = part if y is None else y + part
        x1 = x_ref[rows, :] + _rms(y) * gate_gain
        x1_ref[rows, :] = x1
        h2_ref[rows, :] = (_rms(x1) * ffn_gain + mod_ref[3:4, :]).astype(BF16)


def _outproj_call(x, parts, mod, norm_g, w_out_b, layer, *, latent):
    rows = x.shape[0]
    tm = OUT_TM
    part_spec = pl.BlockSpec((tm, COL_BLOCK), lambda i: (i, 0))
    return pl.pallas_call(
        _outproj_kernel,
        grid=(rows // tm,),
        in_specs=[
            pl.BlockSpec((tm, D_MODEL), lambda i: (i, 0)),
            part_spec, part_spec, part_spec, part_spec,
            _mod_spec(layer, latent, tm, 1),
            pl.BlockSpec((None, 4, D_MODEL), lambda i: (layer, 0, 0)),
            pl.BlockSpec((None, MIX_WIDTH, D_MODEL), lambda i: (layer, 0, 0), pipeline_mode=RESIDENT),
        ],
        out_specs=[
            pl.BlockSpec((tm, D_MODEL), lambda i: (i, 0)),
            pl.BlockSpec((tm, D_MODEL), lambda i: (i, 0)),
        ],
        out_shape=[
            jax.ShapeDtypeStruct((rows, D_MODEL), F32),
            jax.ShapeDtypeStruct((rows, D_MODEL), BF16),
        ],
        compiler_params=_params(("arbitrary",)),
        name="outproj_lat" if latent else "outproj_ctx",
    )(x, *parts, mod, norm_g, w_out_b)


FFN_TM = 512
FFN_OUT_CHUNK = 512
FFN_TH = 512
FFN_SUB = 256


def _ffn_kernel(h_ref, x1_ref, mod_ref, g_ref, wa_ref, wg_ref, wo_ref, o_ref):
    j = pl.program_id(1)

    @pl.when(j == 0)
    def _():
        o_ref[...] = jnp.zeros_like(o_ref)

    h = h_ref[...]
    for s in range(FFN_TH // FFN_SUB):
        sl = slice(s * FFN_SUB, (s + 1) * FFN_SUB)
        a = _dot(h, wa_ref[:, sl])
        g = _dot(h, wg_ref[:, sl])
        t = ((g * jax.nn.sigmoid(g)) * a).astype(BF16)
        for n in range(D_MODEL // FFN_OUT_CHUNK):
            cols = slice(n * FFN_OUT_CHUNK, (n + 1) * FFN_OUT_CHUNK)
            o_ref[:, cols] += _dot(t, wo_ref[sl, cols])

    @pl.when(j == pl.num_programs(1) - 1)
    def _():
        o_ref[...] = x1_ref[...] + _rms(o_ref[...]) * (mod_ref[5:6, :] * g_ref[3:4, :])


def _ffn_call(h2, x1, mod, norm_g, w_ffn_in_b, w_ffn_out_b, layer, *, latent):
    rows = x1.shape[0]
    tm, th = FFN_TM, FFN_TH
    nh = FFN_HIDDEN // th
    return pl.pallas_call(
        _ffn_kernel,
        grid=(rows // tm, nh),
        in_specs=[
            pl.BlockSpec((tm, D_MODEL), lambda i, j: (i, 0)),
            pl.BlockSpec((tm, D_MODEL), lambda i, j: (i, 0)),
            _mod_spec(layer, latent, tm, 2),
            pl.BlockSpec((None, 4, D_MODEL), lambda i, j: (layer, 0, 0)),
            pl.BlockSpec((None, D_MODEL, th), lambda i, j: (layer, 0, j)),
            pl.BlockSpec((None, D_MODEL, th), lambda i, j: (layer, 0, nh + j)),
            pl.BlockSpec((None, th, D_MODEL), lambda i, j: (layer, j, 0)),
        ],
        out_specs=pl.BlockSpec((tm, D_MODEL), lambda i, j: (i, 0)),
        out_shape=jax.ShapeDtypeStruct((rows, D_MODEL), F32),
        compiler_params=_params(("arbitrary", "arbitrary")),
        name="ffn_lat" if latent else "ffn_ctx",
    )(h2, x1, mod, norm_g, w_ffn_in_b, w_ffn_in_b, w_ffn_out_b)


def kernel(x_prompt, x_sample, cache_diff_kv, state_rglru, cache_na_kv, cache_gqa_kv, c, c_ctx, w_mod, b_mod,
           norm_g, w_in, diff_lambda_w, lru_conv_w, lru_conv_b, lru_gate_w, lru_gate_b, lru_lambda, na_rpb,
           gqa_qk_g, w_out, w_ffn_in, w_ffn_out):
    cond = jnp.concatenate([c_ctx[None, :], c, jnp.zeros((MOD_ROWS - 1 - DEC_BATCH, D_MODEL), F32)], axis=0)
    mod = _mod_call(cond, w_mod, b_mod).reshape(DEPTH, MOD_ROWS, 6, D_MODEL)

    w_in_b = w_in.astype(BF16)
    w_out_b = w_out.astype(BF16)
    w_ffn_in_b = w_ffn_in.astype(BF16)
    w_ffn_out_b = w_ffn_out.astype(BF16)
    rope_tabs = _rope_tables()
    na_bias = _na_bias_tables(na_rpb)

    cache_a = cache_diff_kv.reshape(DEC_BATCH, DEPTH, 2, PAST_LEN, DA_HEADS * HEAD_DIM)
    cache_c = cache_na_kv.reshape(DEC_BATCH, DEPTH, 2, PAST_LEN, NA_HEADS * HEAD_DIM)
    cache_d = cache_gqa_kv.reshape(DEC_BATCH, DEPTH, 2, PAST_LEN, GQA_KV_HEADS * HEAD_DIM)
    zero_state = jnp.zeros((BATCH, 2, LRU_WIDTH), F32)

    def layer(x, l, latent, new_caches):
        outs = _inproj_call(x, mod, norm_g, w_in_b, gqa_qk_g, rope_tabs, new_caches, l, latent=latent)
        ub, uf = outs[0], outs[1]
        m_b, st = _lru_call(uf, lru_conv_w, lru_conv_b, lru_gate_w, lru_gate_b, lru_lambda,
                            state_rglru if latent else zero_state, l, latent=latent)
        if latent:
            m_a = _diff_call(ub, diff_lambda_w, cache_a, l)
            m_c = _na_call(ub, cache_c, na_bias, l)
            m_d = _gqa_call(ub, cache_d, l)
        else:
            m_a, m_c, m_d = _attn_ctx_call(ub, diff_lambda_w, l)
        x1, h2 = _outproj_call(x, (m_a, m_b, m_c, m_d), mod, norm_g, w_out_b, l, latent=latent)
        x_new = _ffn_call(h2, x1, mod, norm_g, w_ffn_in_b, w_ffn_out_b, l, latent=latent)
        return x_new, tuple(outs[2:]), st

    y = x_prompt.reshape(BATCH * SEQ, D_MODEL)
    new_caches = tuple(jnp.zeros((BATCH, DEPTH, 2, SEQ, heads, HEAD_DIM), F32)
                       for heads in (DA_HEADS, NA_HEADS, GQA_KV_HEADS))
    states = []
    for l in range(DEPTH):
        y, new_caches, st = layer(y, l, False, new_caches)
        states.append(st)
    y_prompt = y.reshape(BATCH, SEQ, D_MODEL)
    new_cache_diff_kv, new_cache_na_kv, new_cache_gqa_kv = new_caches
    new_state_rglru = jnp.stack(states, axis=1)

    z = x_sample.reshape(DEC_BATCH * DEC_SEQ, D_MODEL)
    for l in range(DEPTH):
        z, _, _ = layer(z, l, True, None)
    y_sample = z.reshape(DEC_BATCH, DEC_SEQ, D_MODEL)

    return (y_prompt, y_sample, new_cache_diff_kv, new_state_rglru, new_cache_na_kv, new_cache_gqa_kv)
```
